```python
import math
import jax
import jax.numpy as jnp
from jax import lax
import numpy as np

D_MODEL = 2048
BATCH = 4
SEQ = 2048
DEPTH = 4

HEAD_DIM = 128
N_GROUPS = 4
HEADS_PER_GROUP = D_MODEL // (N_GROUPS * HEAD_DIM)
GROUP_WIDTH = HEADS_PER_GROUP * HEAD_DIM
DIFF_QK_DIM = HEAD_DIM // 2
Q_BLOCK = 128
MOBA_BLOCK = 256
MOBA_TOPK = 3
MOBA_Q_CHUNK = 32
ROPE_THETA = 10000.0
NORM_EPS = 1e-6
DIFF_SUBLN_EPS = 1e-5
FFN_HIDDEN = -(-8 * D_MODEL // (3 * 256)) * 256
SPLIT_SIZES = (GROUP_WIDTH, GROUP_WIDTH, GROUP_WIDTH, HEADS_PER_GROUP,
               GROUP_WIDTH, GROUP_WIDTH, GROUP_WIDTH,
               GROUP_WIDTH, GROUP_WIDTH, GROUP_WIDTH,
               GROUP_WIDTH, GROUP_WIDTH, GROUP_WIDTH)
IN_COLS = sum(SPLIT_SIZES)

kernel_name = 'hybrid_parallel_heads_fox_moba_diff_stickbreak'


def _rmsnorm(x, g, eps=NORM_EPS):
    xf = x.astype(jnp.float32)
    y = xf * lax.rsqrt(jnp.mean(xf * xf, axis=-1, keepdims=True) + eps)
    return (y * g.astype(jnp.float32)).astype(x.dtype)


def _rope_tables(s_len, dim):
    inv = 1.0 / (ROPE_THETA ** (jnp.arange(0, dim, 2, dtype=jnp.float32) / dim))
    ang = jnp.arange(s_len, dtype=jnp.float32)[:, None] * inv[None, :]
    return jnp.cos(ang), jnp.sin(ang)


def _apply_rope(x, cos, sin):
    x1, x2 = jnp.split(x.astype(jnp.float32), 2, axis=-1)
    out = jnp.concatenate([x1 * cos - x2 * sin, x2 * cos + x1 * sin], axis=-1)
    return out.astype(x.dtype)


def _heads(a):
    b, s, w = a.shape
    return a.reshape(b, s, w // HEAD_DIM, HEAD_DIM).transpose(0, 2, 1, 3)


def _merge(a):
    b, h, s, d = a.shape
    return a.transpose(0, 2, 1, 3).reshape(b, s, h * d)


def _to_blocks(a, blk):
    b, h, s, d = a.shape
    return a.reshape(b, h, s // blk, blk, d).transpose(2, 0, 1, 3, 4)


def _from_blocks(o):
    n, b, h, blk, d = o.shape
    return o.transpose(1, 2, 0, 3, 4).reshape(b, h, n * blk, d)


def _split_cols(z):
    points = [int(p) for p in np.cumsum(SPLIT_SIZES)[:-1]]
    return jnp.split(z, points, axis=-1)


def _forgetting_attention(q, k, v, f_logit):
    s_len, d = q.shape[2], q.shape[3]
    nq = s_len // Q_BLOCK
    c = jnp.cumsum(jax.nn.log_sigmoid(f_logit.astype(jnp.float32)), axis=-1)
    kpos = jnp.arange(s_len)
    scale = d ** -0.5

    def one_block(args):
        qb, cb, q0 = args
        qpos = q0 + jnp.arange(Q_BLOCK)
        logits = jnp.einsum('bhqd,bhkd->bhqk', qb, k).astype(jnp.float32) * scale
        logits = logits + cb[..., :, None] - c[..., None, :]
        logits = jnp.where(kpos[None, :] <= qpos[:, None], logits, -jnp.inf)
        p = jax.nn.softmax(logits, axis=-1)
        return jnp.einsum('bhqk,bhkd->bhqd', p.astype(v.dtype), v)

    c_blocks = c.reshape(c.shape[0], c.shape[1], nq, Q_BLOCK).transpose(2, 0, 1, 3)
    out = lax.map(one_block, (_to_blocks(q, Q_BLOCK), c_blocks, jnp.arange(nq) * Q_BLOCK))
    return _from_blocks(out)


def _moba_attention(q, k, v):
    b, h, s_len, d = q.shape
    n_kb = -(-s_len // MOBA_BLOCK)
    pad = n_kb * MOBA_BLOCK - s_len
    kb = jnp.pad(k, ((0, 0), (0, 0), (0, pad), (0, 0))).reshape(b, h, n_kb, MOBA_BLOCK, d)
    vb = jnp.pad(v, ((0, 0), (0, 0), (0, pad), (0, 0))).reshape(b, h, n_kb, MOBA_BLOCK, d)
    k_mean = jnp.mean(kb.astype(jnp.float32), axis=3)
    top_k = min(MOBA_TOPK, n_kb)
    n_sel = top_k + 1
    blk_ids = jnp.arange(n_kb)
    offs = jnp.arange(MOBA_BLOCK)
    bi = jnp.arange(b)[:, None, None, None]
    hi = jnp.arange(h)[None, :, None, None]
    scale = d ** -0.5

    def one_chunk(args):
        qc, q0 = args
        qpos = q0 + jnp.arange(MOBA_Q_CHUNK)
        own = qpos // MOBA_BLOCK
        own_b = jnp.broadcast_to(own[None, None, :, None], (b, h, MOBA_Q_CHUNK, 1))
        gate = jnp.einsum('bhqd,bhnd->bhqn', qc.astype(jnp.float32), k_mean)
        gate = jnp.where(blk_ids[None, :] < own[:, None], gate, -jnp.inf)
        _, top = lax.top_k(gate, top_k)
        sel = jnp.concatenate([top, own_b], axis=-1)
        slot_ok = jnp.concatenate([top < own_b, jnp.ones_like(own_b, dtype=bool)], axis=-1)
        k_sel = kb[bi, hi, sel]
        v_sel = vb[bi, hi, sel]
        logits = jnp.einsum('bhqd,bhqnkd->bhqnk', qc, k_sel).astype(jnp.float32) * scale
        kpos = sel[..., None] * MOBA_BLOCK + offs
        mask = slot_ok[..., None] & (kpos <= qpos[None, None, :, None, None])
        logits = jnp.where(mask, logits, -jnp.inf).reshape(b, h, MOBA_Q_CHUNK, n_sel * MOBA_BLOCK)
        p = jax.nn.softmax(logits, axis=-1).reshape(b, h, MOBA_Q_CHUNK, n_sel, MOBA_BLOCK)
        return jnp.einsum('bhqnk,bhqnkd->bhqd', p.astype(v_sel.dtype), v_sel)

    nq = s_len // MOBA_Q_CHUNK
    out = lax.map(one_chunk, (_to_blocks(q, MOBA_Q_CHUNK), jnp.arange(nq) * MOBA_Q_CHUNK))
    return _from_blocks(out)


def _differential_attention(q, k, v, lam, lam_init, g_sub, cos, sin):
    s_len = q.shape[2]
    nq = s_len // Q_BLOCK
    q1 = _apply_rope(q[..., :DIFF_QK_DIM], cos, sin)
    q2 = _apply_rope(q[..., DIFF_QK_DIM:], cos, sin)
    k1 = _apply_rope(k[..., :DIFF_QK_DIM], cos, sin)
    k2 = _apply_rope(k[..., DIFF_QK_DIM:], cos, sin)
    kpos = jnp.arange(s_len)
    scale = DIFF_QK_DIM ** -0.5

    def one_block(args):
        q1b, q2b, q0 = args
        qpos = q0 + jnp.arange(Q_BLOCK)
        causal = kpos[None, :] <= qpos[:, None]
        l1 = jnp.einsum('bhqd,bhkd->bhqk', q1b, k1).astype(jnp.float32) * scale
        l2 = jnp.einsum('bhqd,bhkd->bhqk', q2b, k2).astype(jnp.float32) * scale
        p1 = jax.nn.softmax(jnp.where(causal, l1, -jnp.inf), axis=-1)
        p2 = jax.nn.softmax(jnp.where(causal, l2, -jnp.inf), axis=-1)
        p = p1 - lam * p2
        return jnp.einsum('bhqk,bhkd->bhqd', p.astype(v.dtype), v)

    out = lax.map(one_block, (_to_blocks(q1, Q_BLOCK), _to_blocks(q2, Q_BLOCK), jnp.arange(nq) * Q_BLOCK))
    out = _from_blocks(out)
    return _rmsnorm(out, g_sub, DIFF_SUBLN_EPS) * (1.0 - lam_init)


def _stick_breaking_attention(q, k, v):
    s_len, d = q.shape[2], q.shape[3]
    nq = s_len // Q_BLOCK
    kpos = jnp.arange(s_len)
    scale = d ** -0.5

    def one_block(args):
        qb, q0 = args
        qpos = q0 + jnp.arange(Q_BLOCK)
        z = jnp.einsum('bhqd,bhkd->bhqk', qb, k).astype(jnp.float32) * scale
        strict = kpos[None, :] < qpos[:, None]
        log_beta = jax.nn.log_sigmoid(z)
        log_1m = jnp.where(strict, jax.nn.log_sigmoid(-z), 0.0)
        between = lax.cumsum(log_1m, axis=3, reverse=True) - log_1m
        a = jnp.where(strict, jnp.exp(log_beta + between), 0.0)
        return jnp.einsum('bhqk,bhkd->bhqd', a.astype(v.dtype), v)

    out = lax.map(one_block, (_to_blocks(q, Q_BLOCK), jnp.arange(nq) * Q_BLOCK))
    return _from_blocks(out)


def setup_inputs(seed: int = 0) -> dict:
    key = jax.random.key(seed)
    ks = jax.random.split(key, 16)
    f32 = jnp.float32

    def nrm(k, shape, scale):
        return jax.random.normal(k, shape, f32) * scale

    return {
        'x': nrm(ks[0], (BATCH, SEQ, D_MODEL), 1.0),
        'w_in': nrm(ks[1], (DEPTH, D_MODEL, IN_COLS), D_MODEL ** -0.5),
        'b_fgate': nrm(ks[2], (DEPTH, HEADS_PER_GROUP), 0.1),
        'w_out': nrm(ks[3], (DEPTH, D_MODEL, D_MODEL), D_MODEL ** -0.5),
        'diff_lq1': nrm(ks[4], (DEPTH, DIFF_QK_DIM), 0.1),
        'diff_lk1': nrm(ks[5], (DEPTH, DIFF_QK_DIM), 0.1),
        'diff_lq2': nrm(ks[6], (DEPTH, DIFF_QK_DIM), 0.1),
        'diff_lk2': nrm(ks[7], (DEPTH, DIFF_QK_DIM), 0.1),
        'diff_subln': 1.0 + nrm(ks[8], (DEPTH, HEAD_DIM), 0.02),
        'attn_norm': 1.0 + nrm(ks[9], (DEPTH, D_MODEL), 0.02),
        'w_gate': nrm(ks[10], (DEPTH, D_MODEL, FFN_HIDDEN), D_MODEL ** -0.5),
        'w_up': nrm(ks[11], (DEPTH, D_MODEL, FFN_HIDDEN), D_MODEL ** -0.5),
        'w_down': nrm(ks[12], (DEPTH, FFN_HIDDEN, D_MODEL), FFN_HIDDEN ** -0.5),
        'ffn_norm': 1.0 + nrm(ks[13], (DEPTH, D_MODEL), 0.02),
        'final_norm': 1.0 + nrm(ks[14], (D_MODEL,), 0.02),
    }


def reference(x, w_in, b_fgate, w_out, diff_lq1, diff_lk1, diff_lq2, diff_lk2, diff_subln,
              attn_norm, w_gate, w_up, w_down, ffn_norm, final_norm):
    s_len = x.shape[1]
    cos_full, sin_full = _rope_tables(s_len, HEAD_DIM)
    cos_half, sin_half = _rope_tables(s_len, DIFF_QK_DIM)
    for l in range(DEPTH):
        h = _rmsnorm(x, attn_norm[l])
        z = jnp.einsum('bsd,dc->bsc', h, w_in[l])
        (fq, fk, fv, fg, mq, mk, mv, dq, dk, dv, sq, sk, sv) = _split_cols(z)

        fox = _forgetting_attention(_heads(fq), _heads(fk), _heads(fv),
                                    (fg + b_fgate[l]).transpose(0, 2, 1))

        moba = _moba_attention(_apply_rope(_heads(mq), cos_full, sin_full),
                               _apply_rope(_heads(mk), cos_full, sin_full), _heads(mv))

        lam_init = 0.8 - 0.6 * math.exp(-0.3 * l)
        lam = (jnp.exp(jnp.sum(diff_lq1[l].astype(jnp.float32) * diff_lk1[l].astype(jnp.float32)))
               - jnp.exp(jnp.sum(diff_lq2[l].astype(jnp.float32) * diff_lk2[l].astype(jnp.float32)))
               + lam_init)
        diff = _differential_attention(_heads(dq), _heads(dk), _heads(dv), lam, lam_init,
                                       diff_subln[l], cos_half, sin_half)

        sb = _stick_breaking_attention(_heads(sq), _heads(sk), _heads(sv))

        mixed = jnp.concatenate([_merge(fox), _merge(moba), _merge(diff), _merge(sb)], axis=-1)
        x = x + jnp.einsum('bsc,cd->bsd', mixed, w_out[l])

        h = _rmsnorm(x, ffn_norm[l])
        act = jax.nn.silu(jnp.einsum('bsd,df->bsf', h, w_gate[l])) * jnp.einsum('bsd,df->bsf', h, w_up[l])
        x = x + jnp.einsum('bsf,fd->bsd', act, w_down[l])
    return _rmsnorm(x, final_norm)
```

```python
import functools
import math

import jax
import jax.numpy as jnp
from jax import lax
from jax.experimental import pallas as pl
from jax.experimental.pallas import tpu as pltpu

F32 = jnp.float32
BF16 = jnp.bfloat16

HEAD_DIM = 128
HEADS_PER_GROUP = 4
GROUP_WIDTH = HEADS_PER_GROUP * HEAD_DIM
N_PROJ = 12
DIFF_QK_DIM = HEAD_DIM // 2
MOBA_BLOCK = 256
MOBA_TOPK = 3
ROPE_THETA = 10000.0
NORM_EPS = 1e-6
DIFF_SUBLN_EPS = 1e-5
LANES = 128
NEG_INF = float("-inf")

VMEM_LIMIT_BYTES = 56 * 1024 * 1024

(T_FQ, T_FK, T_FV, T_MQ, T_MK, T_MV, T_DQ, T_DK, T_DV, T_SQ, T_SK, T_SV) = range(N_PROJ)


def _nt_dot(a, b):
    return lax.dot_general(a, b, (((1,), (1,)), ((), ())), preferred_element_type=F32)


def _dot(a, b):
    return jnp.dot(a, b, preferred_element_type=F32)


def _log1p_exp_neg_abs(z):
    return jnp.log1p(jnp.exp(-jnp.abs(z)))


def _split3_bf16(x):
    hi = x.astype(BF16)
    r = x - hi.astype(F32)
    mid = r.astype(BF16)
    lo = (r - mid.astype(F32)).astype(BF16)
    return hi, mid, lo


def _inproj_kernel(x_ref, g_ref, w_ref, wfg_ref, cf_ref, sf_ref, ch_ref, sha_ref, shb_ref,
                   z_ref, fg_ref, h_scr):
    j = pl.program_id(1)

    @pl.when(j == 0)
    def _():
        x = x_ref[...]
        ms = jnp.mean(x * x, axis=-1, keepdims=True)
        h = (x * lax.rsqrt(ms + NORM_EPS) * g_ref[...]).astype(BF16)
        h_scr[...] = h
        fg_ref[...] = _dot(h, wfg_ref[...])

    acc = _dot(h_scr[...], w_ref[...])
    scale_full = HEAD_DIM ** -0.5
    scale_half = DIFF_QK_DIM ** -0.5

    def slabs():
        for s in range(HEADS_PER_GROUP):
            yield slice(s * HEAD_DIM, (s + 1) * HEAD_DIM)

    def rope_full(a):
        return a * cf_ref[...] + pltpu.roll(a, HEAD_DIM // 2, 1) * sf_ref[...]

    def rope_half(a):
        return (a * ch_ref[...] + pltpu.roll(a, HEAD_DIM - DIFF_QK_DIM // 2, 1) * sha_ref[...]
                + pltpu.roll(a, DIFF_QK_DIM // 2, 1) * shb_ref[...])

    @pl.when((j == T_FQ) | (j == T_SQ))
    def _():
        z_ref[...] = (acc * scale_full).astype(BF16)

    @pl.when(j == T_MQ)
    def _():
        for sl in slabs():
            z_ref[:, sl] = (rope_full(acc[:, sl]) * scale_full).astype(BF16)

    @pl.when(j == T_MK)
    def _():
        for sl in slabs():
            z_ref[:, sl] = rope_full(acc[:, sl]).astype(BF16)

    @pl.when(j == T_DQ)
    def _():
        for sl in slabs():
            z_ref[:, sl] = (rope_half(acc[:, sl]) * scale_half).astype(BF16)

    @pl.when(j == T_DK)
    def _():
        for sl in slabs():
            z_ref[:, sl] = rope_half(acc[:, sl]).astype(BF16)

    plain = ((j == T_FK) | (j == T_FV) | (j == T_MV) | (j == T_DV) | (j == T_SK) | (j == T_SV))

    @pl.when(plain)
    def _():
        z_ref[...] = acc.astype(BF16)


def _inproj(x2, g, w, wfg, tables, seq, tm):
    m, d = x2.shape
    tm = min(tm, seq)
    npos = seq // tm
    n = w.shape[1]
    tab_spec = pl.BlockSpec((tm, LANES), lambda i, j: (i % npos, 0))
    return pl.pallas_call(
        _inproj_kernel,
        grid=(m // tm, n // GROUP_WIDTH),
        in_specs=[
            pl.BlockSpec((tm, d), lambda i, j: (i, 0)),
            pl.BlockSpec((1, d), lambda i, j: (0, 0)),
            pl.BlockSpec((d, GROUP_WIDTH), lambda i, j: (0, j)),
            pl.BlockSpec((d, LANES), lambda i, j: (0, 0)),
            tab_spec, tab_spec, tab_spec, tab_spec, tab_spec,
        ],
        out_specs=[
            pl.BlockSpec((tm, GROUP_WIDTH), lambda i, j: (i, j)),
            pl.BlockSpec((tm, LANES), lambda i, j: (i, 0)),
        ],
        out_shape=[
            jax.ShapeDtypeStruct((m, n), BF16),
            jax.ShapeDtypeStruct((m, LANES), F32),
        ],
        scratch_shapes=[pltpu.VMEM((tm, d), BF16)],
        compiler_params=pltpu.CompilerParams(
            dimension_semantics=("parallel", "arbitrary"), vmem_limit_bytes=VMEM_LIMIT_BYTES),
        name="inproj",
    )(x2, g, w, wfg, *tables)


def _fgate_cumsum_kernel(fg_ref, b_ref, c_ref, *, blk):
    s_len = fg_ref.shape[1]
    row = lax.broadcasted_iota(jnp.int32, (blk, blk), 0)
    col = lax.broadcasted_iota(jnp.int32, (blk, blk), 1)
    tri = jnp.where(row >= col, 1.0, 0.0).astype(BF16)
    carry = jnp.zeros((1, LANES), F32)
    for n in range(s_len // blk):
        x = fg_ref[0, n * blk:(n + 1) * blk, :] + b_ref[...]
        ls = jnp.minimum(x, 0.0) - _log1p_exp_neg_abs(x)
        hi, mid, lo = _split3_bf16(ls)
        cb = _dot(tri, hi) + _dot(tri, mid) + _dot(tri, lo) + carry
        c_ref[0, n * blk:(n + 1) * blk, :] = cb
        carry = cb[blk - 1:blk, :]


def _fgate_cumsum(fg3, bias):
    b, s_len, _ = fg3.shape
    blk = min(256, s_len)
    return pl.pallas_call(
        functools.partial(_fgate_cumsum_kernel, blk=blk),
        grid=(b,),
        in_specs=[pl.BlockSpec((1, s_len, LANES), lambda i: (i, 0, 0)),
                  pl.BlockSpec((1, LANES), lambda i: (0, 0))],
        out_specs=pl.BlockSpec((1, s_len, LANES), lambda i: (i, 0, 0)),
        out_shape=jax.ShapeDtypeStruct((b, s_len, LANES), F32),
        compiler_params=pltpu.CompilerParams(dimension_semantics=("parallel",)),
        name="fgate_cumsum",
    )(fg3, bias)


def _kv_tile(k_ref, v_ref, j, tk):
    start = pl.multiple_of(j * tk, tk)
    return k_ref[0, pl.ds(start, tk), :], v_ref[0, pl.ds(start, tk), :], start


def _softmax_step(s, v, carry):
    m, l, acc = carry
    m_new = jnp.maximum(m, jnp.max(s, axis=-1, keepdims=True))
    alpha = jnp.exp(m - m_new)
    p = jnp.exp(s - m_new)
    l = alpha * l + jnp.sum(p, axis=-1, keepdims=True)
    acc = alpha * acc + _dot(p.astype(BF16), v)
    return m_new, l, acc


def _softmax_init(tq):
    return (jnp.full((tq, 1), NEG_INF, F32), jnp.zeros((tq, 1), F32), jnp.zeros((tq, HEAD_DIM), F32))


def _causal_mask(tq, tk, strict=False):
    row = lax.broadcasted_iota(jnp.int32, (tq, tk), 0)
    col = lax.broadcasted_iota(jnp.int32, (tq, tk), 1)
    return (col < row) if strict else (col <= row)


def _fox_kernel(q_ref, k_ref, v_ref, c_ref, ct_ref, o_ref, *, t):
    h = pl.program_id(1)
    i = pl.program_id(2)
    q = q_ref[0]
    lane = lax.broadcasted_iota(jnp.int32, (t, LANES), 1)
    cq = jnp.sum(jnp.where(lane == h, c_ref[0], 0.0), axis=-1, keepdims=True)

    def logits(j):
        k, v, start = _kv_tile(k_ref, v_ref, j, t)
        ck = ct_ref[0, pl.ds(h, 1), pl.ds(start, t)]
        return _nt_dot(q, k) + (cq - ck), v

    def body(j, carry):
        s, v = logits(j)
        return _softmax_step(s, v, carry)

    carry = lax.fori_loop(0, i, body, _softmax_init(t))
    s, v = logits(i)
    s = jnp.where(_causal_mask(t, t), s, NEG_INF)
    _, l, acc = _softmax_step(s, v, carry)
    o_ref[0] = (acc / l).astype(o_ref.dtype)


def _moba_kernel(q_ref, k_ref, v_ref, o_ref, *, t):
    i = pl.program_id(2)
    s_len = k_ref.shape[1]
    q = q_ref[0]

    brow = lax.broadcasted_iota(jnp.int32, (LANES, s_len), 0)
    bcol = lax.broadcasted_iota(jnp.int32, (LANES, s_len), 1)
    member = jnp.where(bcol // MOBA_BLOCK == brow, 1.0, 0.0).astype(BF16)
    k_mean = _dot(member, k_ref[0]) * (1.0 / MOBA_BLOCK)
    hi, mid, lo = _split3_bf16(k_mean)
    gate = _nt_dot(q, hi) + _nt_dot(q, mid) + _nt_dot(q, lo)

    lane = lax.broadcasted_iota(jnp.int32, (t, LANES), 1)
    past = lane < i
    gate = jnp.where(past, gate, NEG_INF)
    beaten = jnp.zeros((t, LANES), F32)
    for mblk in range(s_len // MOBA_BLOCK):
        gm = gate[:, mblk:mblk + 1]
        wins = jnp.where(gm > gate, 1.0, jnp.where((gm == gate) & (mblk < lane), 1.0, 0.0))
        beaten = beaten + wins
    keep = jnp.where(past & (beaten < MOBA_TOPK), 1.0, 0.0)

    k, v, _ = _kv_tile(k_ref, v_ref, i, t)
    s = jnp.where(_causal_mask(t, t), _nt_dot(q, k), NEG_INF)
    carry = _softmax_step(s, v, _softmax_init(t))

    def body(j, carry):
        k, v, _ = _kv_tile(k_ref, v_ref, j, t)
        keep_j = jnp.max(jnp.where(lane == j, keep, 0.0), axis=-1, keepdims=True)
        s = jnp.where(keep_j > 0.0, _nt_dot(q, k), NEG_INF)
        return _softmax_step(s, v, carry)

    _, l, acc = lax.fori_loop(0, i, body, carry)
    o_ref[0] = (acc / l).astype(o_ref.dtype)


def _diff_kernel(q_ref, k_ref, v_ref, lq1_ref, lk1_ref, lq2_ref, lk2_ref, gsub_ref, o_ref, *,
                 t, lam_init):
    i = pl.program_id(2)
    lam = (jnp.exp(jnp.sum(lq1_ref[...] * lk1_ref[...], axis=-1, keepdims=True))
           - jnp.exp(jnp.sum(lq2_ref[...] * lk2_ref[...], axis=-1, keepdims=True)) + lam_init)
    q = q_ref[0]
    lane = lax.broadcasted_iota(jnp.int32, (t, LANES), 1)
    zero = jnp.zeros_like(q)
    q1 = jnp.where(lane < DIFF_QK_DIM, q, zero)
    q2 = jnp.where(lane >= DIFF_QK_DIM, q, zero)

    def step(j, carry, mask):
        c1, c2 = carry
        k, v, _ = _kv_tile(k_ref, v_ref, j, t)
        s1 = _nt_dot(q1, k)
        s2 = _nt_dot(q2, k)
        if mask is not None:
            s1 = jnp.where(mask, s1, NEG_INF)
            s2 = jnp.where(mask, s2, NEG_INF)
        return _softmax_step(s1, v, c1), _softmax_step(s2, v, c2)

    carry = lax.fori_loop(0, i, lambda j, c: step(j, c, None), (_softmax_init(t), _softmax_init(t)))
    (_, l1, a1), (_, l2, a2) = step(i, carry, _causal_mask(t, t))
    out = a1 / l1 - lam * (a2 / l2)
    ms = jnp.mean(out * out, axis=-1, keepdims=True)
    y = out * lax.rsqrt(ms + DIFF_SUBLN_EPS) * gsub_ref[...]
    o_ref[0] = (y * (1.0 - lam_init)).astype(o_ref.dtype)


def _sb_kernel(q_ref, k_ref, v_ref, o_ref, *, t):
    i = pl.program_id(2)
    q = q_ref[0]
    row = lax.broadcasted_iota(jnp.int32, (t, t), 0)
    col = lax.broadcasted_iota(jnp.int32, (t, t), 1)
    later = jnp.where(row > col, 1.0, 0.0).astype(BF16)

    def step(j, carry, strict):
        run, acc = carry
        k, v, _ = _kv_tile(k_ref, v_ref, j, t)
        z = _nt_dot(q, k)
        sp = _log1p_exp_neg_abs(z)
        log_beta = jnp.minimum(z, 0.0) - sp
        log_1m = jnp.minimum(-z, 0.0) - sp
        if strict is not None:
            log_1m = jnp.where(strict, log_1m, 0.0)
        hi, mid, lo = _split3_bf16(log_1m)
        between = _dot(hi, later) + _dot(mid, later) + _dot(lo, later) + run
        a = jnp.exp(log_beta + between)
        if strict is not None:
            a = jnp.where(strict, a, 0.0)
        acc = acc + _dot(a.astype(BF16), v)
        run = run + jnp.sum(log_1m, axis=-1, keepdims=True)
        return run, acc

    carry = step(i, (jnp.zeros((t, 1), F32), jnp.zeros((t, HEAD_DIM), F32)),
                 _causal_mask(t, t, strict=True))
    _, acc = lax.fori_loop(0, i, lambda jj, c: step(i - 1 - jj, c, None), carry)
    o_ref[0] = acc.astype(o_ref.dtype)


def _attention_call(kernel_fn, name, z3, tile_q, extra_inputs=(), extra_specs=(), t=256):
    b, s_len, _ = z3.shape
    t = min(t, s_len)
    hq, hk, hv = (tile_q * HEADS_PER_GROUP, (tile_q + 1) * HEADS_PER_GROUP, (tile_q + 2) * HEADS_PER_GROUP)
    return pl.pallas_call(
        functools.partial(kernel_fn, t=t),
        grid=(b, HEADS_PER_GROUP, s_len // t),
        in_specs=[
            pl.BlockSpec((1, t, HEAD_DIM), lambda bi, h, i: (bi, i, hq + h)),
            pl.BlockSpec((1, s_len, HEAD_DIM), lambda bi, h, i: (bi, 0, hk + h)),
            pl.BlockSpec((1, s_len, HEAD_DIM), lambda bi, h, i: (bi, 0, hv + h)),
            *extra_specs,
        ],
        out_specs=pl.BlockSpec((1, t, HEAD_DIM), lambda bi, h, i: (bi, i, h)),
        out_shape=jax.ShapeDtypeStruct((b, s_len, GROUP_WIDTH), BF16),
        compiler_params=pltpu.CompilerParams(
            dimension_semantics=("parallel", "parallel", "parallel"),
            vmem_limit_bytes=VMEM_LIMIT_BYTES),
        name=name,
    )(z3, z3, z3, *extra_inputs)


def _fox_attention(z3, c, c_t):
    s_len = z3.shape[1]
    t = min(256, s_len)
    return _attention_call(
        _fox_kernel, "fox", z3, T_FQ, (c, c_t),
        (pl.BlockSpec((1, t, LANES), lambda bi, h, i: (bi, i, 0)),
         pl.BlockSpec((1, 8, s_len), lambda bi, h, i: (bi, 0, 0))))


def _moba_attention(z3):
    assert z3.shape[1] % MOBA_BLOCK == 0 and z3.shape[1] // MOBA_BLOCK <= LANES
    return _attention_call(_moba_kernel, "moba", z3, T_MQ, t=MOBA_BLOCK)


def _diff_attention(z3, lq1, lk1, lq2, lk2, gsub, lam_init):
    vec = lambda n: pl.BlockSpec((1, n), lambda bi, h, i: (0, 0))
    return _attention_call(
        functools.partial(_diff_kernel, lam_init=lam_init), "diff", z3, T_DQ,
        (lq1, lk1, lq2, lk2, gsub),
        (vec(DIFF_QK_DIM), vec(DIFF_QK_DIM), vec(DIFF_QK_DIM), vec(DIFF_QK_DIM), vec(HEAD_DIM)))


def _sb_attention(z3):
    return _attention_call(_sb_kernel, "stickbreak", z3, T_SQ)


def _outproj_kernel(a0_ref, a1_ref, a2_ref, a3_ref, w_ref, x_ref, o_ref):
    acc = x_ref[...]
    for g, a_ref in enumerate((a0_ref, a1_ref, a2_ref, a3_ref)):
        acc = acc + _dot(a_ref[...], w_ref[g * GROUP_WIDTH:(g + 1) * GROUP_WIDTH, :])
    o_ref[...] = acc


def _outproj(mixed, w, x2, tm=1024, tn=512):
    m, d = x2.shape
    tm = min(tm, m)
    a_spec = pl.BlockSpec((tm, GROUP_WIDTH), lambda i, j: (i, 0))
    return pl.pallas_call(
        _outproj_kernel,
        grid=(m // tm, d // tn),
        in_specs=[a_spec, a_spec, a_spec, a_spec,
                  pl.BlockSpec((w.shape[0], tn), lambda i, j: (0, j)),
                  pl.BlockSpec((tm, tn), lambda i, j: (i, j))],
        out_specs=pl.BlockSpec((tm, tn), lambda i, j: (i, j)),
        out_shape=jax.ShapeDtypeStruct((m, d), F32),
        compiler_params=pltpu.CompilerParams(
            dimension_semantics=("parallel", "parallel"), vmem_limit_bytes=VMEM_LIMIT_BYTES),
        name="outproj",
    )(*mixed, w, x2)


def _ffn_up_kernel(x_ref, g_ref, wg_ref, wu_ref, o_ref, h_scr):
    @pl.when(pl.program_id(1) == 0)
    def _():
        x = x_ref[...]
        ms = jnp.mean(x * x, axis=-1, keepdims=True)
        h_scr[...] = (x * lax.rsqrt(ms + NORM_EPS) * g_ref[...]).astype(BF16)

    h = h_scr[...]
    gate = _dot(h, wg_ref[...])
    up = _dot(h, wu_ref[...])
    o_ref[...] = (gate / (1.0 + jnp.exp(-gate)) * up).astype(BF16)


def _ffn_up(x2, g, wg, wu, tm=1024, tn=512):
    m, d = x2.shape
    f = wg.shape[1]
    tm = min(tm, m)
    return pl.pallas_call(
        _ffn_up_kernel,
        grid=(m // tm, f // tn),
        in_specs=[pl.BlockSpec((tm, d), lambda i, j: (i, 0)),
                  pl.BlockSpec((1, d), lambda i, j: (0, 0)),
                  pl.BlockSpec((d, tn), lambda i, j: (0, j)),
                  pl.BlockSpec((d, tn), lambda i, j: (0, j))],
        out_specs=pl.BlockSpec((tm, tn), lambda i, j: (i, j)),
        out_shape=jax.ShapeDtypeStruct((m, f), BF16),
        scratch_shapes=[pltpu.VMEM((tm, d), BF16)],
        compiler_params=pltpu.CompilerParams(
            dimension_semantics=("parallel", "arbitrary"), vmem_limit_bytes=VMEM_LIMIT_BYTES),
        name="ffn_up",
    )(x2, g, wg, wu)


def _ffn_down_kernel(a_ref, w_ref, x_ref, o_ref):
    o_ref[...] = x_ref[...] + _dot(a_ref[...], w_ref[...])


def _ffn_down(act, w, x2, tm=512, tn=512):
    m, d = x2.shape
    f = act.shape[1]
    tm = min(tm, m)
    return pl.pallas_call(
        _ffn_down_kernel,
        grid=(m // tm, d // tn),
        in_specs=[pl.BlockSpec((tm, f), lambda i, j: (i, 0)),
                  pl.BlockSpec((f, tn), lambda i, j: (0, j)),
                  pl.BlockSpec((tm, tn), lambda i, j: (i, j))],
        out_specs=pl.BlockSpec((tm, tn), lambda i, j: (i, j)),
        out_shape=jax.ShapeDtypeStruct((m, d), F32),
        compiler_params=pltpu.CompilerParams(
            dimension_semantics=("parallel", "parallel"), vmem_limit_bytes=VMEM_LIMIT_BYTES),
        name="ffn_down",
    )(act, w, x2)


def _final_norm_kernel(x_ref, g_ref, o_ref):
    x = x_ref[...]
    ms = jnp.mean(x * x, axis=-1, keepdims=True)
    o_ref[...] = x * lax.rsqrt(ms + NORM_EPS) * g_ref[...]


def _final_norm(x2, g, tm=512):
    m, d = x2.shape
    tm = min(tm, m)
    return pl.pallas_call(
        _final_norm_kernel,
        grid=(m // tm,),
        in_specs=[pl.BlockSpec((tm, d), lambda i: (i, 0)), pl.BlockSpec((1, d), lambda i: (0, 0))],
        out_specs=pl.BlockSpec((tm, d), lambda i: (i, 0)),
        out_shape=jax.ShapeDtypeStruct((m, d), F32),
        compiler_params=pltpu.CompilerParams(dimension_semantics=("parallel",)),
        name="final_norm",
    )(x2, g)


def _rope_tables(s_len):
    def cos_sin(dim):
        inv = 1.0 / (ROPE_THETA ** (jnp.arange(0, dim, 2, dtype=F32) / dim))
        ang = jnp.arange(s_len, dtype=F32)[:, None] * inv[None, :]
        return jnp.cos(ang), jnp.sin(ang)

    cf, sf = cos_sin(HEAD_DIM)
    ch, sh = cos_sin(DIFF_QK_DIM)
    zh = jnp.zeros_like(sh)
    return (jnp.concatenate([cf, cf], axis=-1),
            jnp.concatenate([-sf, sf], axis=-1),
            jnp.concatenate([ch, ch, ch, ch], axis=-1),
            jnp.concatenate([-sh, zh, -sh, zh], axis=-1),
            jnp.concatenate([zh, sh, zh, sh], axis=-1))


def _reorder_w_in(w_in):
    gw = GROUP_WIDTH
    fg0 = 3 * gw
    w_main = jnp.concatenate([w_in[..., :fg0], w_in[..., fg0 + HEADS_PER_GROUP:]], axis=-1)
    w_fg = jnp.pad(w_in[..., fg0:fg0 + HEADS_PER_GROUP], ((0, 0), (0, 0), (0, LANES - HEADS_PER_GROUP)))
    return w_main.astype(BF16), w_fg.astype(BF16)


def kernel(x, w_in, b_fgate, w_out, diff_lq1, diff_lk1, diff_lq2, diff_lk2, diff_subln, attn_norm,
           w_gate, w_up, w_down, ffn_norm, final_norm):
    b, s_len, d = x.shape
    depth = w_in.shape[0]
    m = b * s_len
    tables = _rope_tables(s_len)
    w_main, w_fg = _reorder_w_in(w_in)
    w_out_b = w_out.astype(BF16)
    w_gate_b = w_gate.astype(BF16)
    w_up_b = w_up.astype(BF16)
    w_down_b = w_down.astype(BF16)
    b_fg = jnp.pad(b_fgate, ((0, 0), (0, LANES - HEADS_PER_GROUP)))

    x2 = x.reshape(m, d)
    for l in range(depth):
        z, fg = _inproj(x2, attn_norm[l][None, :], w_main[l], w_fg[l], tables, s_len, tm=1024)
        z3 = z.reshape(b, s_len, N_PROJ * GROUP_WIDTH)
        c = _fgate_cumsum(fg.reshape(b, s_len, LANES), b_fg[l][None, :])
        c_t = jnp.swapaxes(c[:, :, :8], 1, 2)
        lam_init = 0.8 - 0.6 * math.exp(-0.3 * l)
        fox = _fox_attention(z3, c, c_t)
        moba = _moba_attention(z3)
        diff = _diff_attention(z3, diff_lq1[l][None, :], diff_lk1[l][None, :], diff_lq2[l][None, :],
                               diff_lk2[l][None, :], diff_subln[l][None, :], lam_init)
        sb = _sb_attention(z3)
        mixed = [a.reshape(m, GROUP_WIDTH) for a in (fox, moba, diff, sb)]
        x2 = _outproj(mixed, w_out_b[l], x2)
        act = _ffn_up(x2, ffn_norm[l][None, :], w_gate_b[l], w_up_b[l])
        x2 = _ffn_down(act, w_down_b[l], x2)
    return _final_norm(x2, final_norm[None, :]).reshape(b, s_len, d)
```

```python
import functools
import math

import jax
import jax.numpy as jnp
from jax import lax
from jax.experimental import pallas as pl
from jax.experimental.pallas import tpu as pltpu

F32 = jnp.float32
BF16 = jnp.bfloat16

HEAD_DIM = 128
HEADS_PER_GROUP = 4
GROUP_WIDTH = HEADS_PER_GROUP * HEAD_DIM
N_PROJ = 12
DIFF_QK_DIM = HEAD_DIM // 2
MOBA_BLOCK = 256
MOBA_TOPK = 3
ROPE_THETA = 10000.0
NORM_EPS = 1e-6
DIFF_SUBLN_EPS = 1e-5
LANES = 128
SUBLANES = 8
NEG_INF = float("-inf")
LOG2E = math.log2(math.e)
ATTN_TILE = 256

VMEM_LIMIT_BYTES = 56 * 1024 * 1024

(T_FQ, T_FK, T_FV, T_MQ, T_MK, T_MV, T_DQ, T_DK, T_DV, T_SQ, T_SK, T_SV) = range(N_PROJ)


def _nt_dot(a, b):
    return lax.dot_general(a, b, (((1,), (1,)), ((), ())), preferred_element_type=F32)


def _dot(a, b):
    return jnp.dot(a, b, preferred_element_type=F32)


def _split2_bf16(x):
    hi = x.astype(BF16)
    lo = (x - hi.astype(F32)).astype(BF16)
    return hi, lo


def _split3_bf16(x):
    hi = x.astype(BF16)
    r = x - hi.astype(F32)
    mid = r.astype(BF16)
    lo = (r - mid.astype(F32)).astype(BF16)
    return hi, mid, lo


def _inproj_kernel(x_ref, g_ref, w_ref, wfg_ref, cf_ref, sf_ref, ch_ref, sha_ref, shb_ref,
                   z_ref, fg_ref, h_scr):
    j = pl.program_id(1)

    @pl.when(j == 0)
    def _():
        x = x_ref[...]
        ms = jnp.mean(x * x, axis=-1, keepdims=True)
        h = (x * lax.rsqrt(ms + NORM_EPS) * g_ref[...]).astype(BF16)
        h_scr[...] = h
        fg_ref[...] = _dot(h, wfg_ref[...])

    acc = _dot(h_scr[...], w_ref[...])
    scale_full = HEAD_DIM ** -0.5 * LOG2E
    scale_half = DIFF_QK_DIM ** -0.5 * LOG2E

    def slabs():
        for s in range(HEADS_PER_GROUP):
            yield slice(s * HEAD_DIM, (s + 1) * HEAD_DIM)

    def rope_full(a):
        return a * cf_ref[...] + pltpu.roll(a, HEAD_DIM // 2, 1) * sf_ref[...]

    def rope_half(a):
        return (a * ch_ref[...] + pltpu.roll(a, HEAD_DIM - DIFF_QK_DIM // 2, 1) * sha_ref[...]
                + pltpu.roll(a, DIFF_QK_DIM // 2, 1) * shb_ref[...])

    @pl.when((j == T_FQ) | (j == T_SQ))
    def _():
        z_ref[...] = (acc * scale_full).astype(BF16)

    @pl.when(j == T_MQ)
    def _():
        for sl in slabs():
            z_ref[:, sl] = (rope_full(acc[:, sl]) * scale_full).astype(BF16)

    @pl.when(j == T_MK)
    def _():
        for sl in slabs():
            z_ref[:, sl] = rope_full(acc[:, sl]).astype(BF16)

    @pl.when(j == T_DQ)
    def _():
        for sl in slabs():
            z_ref[:, sl] = (rope_half(acc[:, sl]) * scale_half).astype(BF16)

    @pl.when(j == T_DK)
    def _():
        for sl in slabs():
            z_ref[:, sl] = rope_half(acc[:, sl]).astype(BF16)

    plain = ((j == T_FK) | (j == T_FV) | (j == T_MV) | (j == T_DV) | (j == T_SK) | (j == T_SV))

    @pl.when(plain)
    def _():
        z_ref[...] = acc.astype(BF16)


def _inproj(x2, g, w, wfg, tables, seq, tm):
    m, d = x2.shape
    tm = min(tm, seq)
    npos = seq // tm
    n = w.shape[1]
    tab_spec = pl.BlockSpec((tm, LANES), lambda i, j: (i % npos, 0))
    return pl.pallas_call(
        _inproj_kernel,
        grid=(m // tm, n // GROUP_WIDTH),
        in_specs=[
            pl.BlockSpec((tm, d), lambda i, j: (i, 0)),
            pl.BlockSpec((1, d), lambda i, j: (0, 0)),
            pl.BlockSpec((d, GROUP_WIDTH), lambda i, j: (0, j)),
            pl.BlockSpec((d, LANES), lambda i, j: (0, 0)),
            tab_spec, tab_spec, tab_spec, tab_spec, tab_spec,
        ],
        out_specs=[
            pl.BlockSpec((tm, GROUP_WIDTH), lambda i, j: (i, j)),
            pl.BlockSpec((tm, LANES), lambda i, j: (i, 0)),
        ],
        out_shape=[
            jax.ShapeDtypeStruct((m, n), BF16),
            jax.ShapeDtypeStruct((m, LANES), F32),
        ],
        scratch_shapes=[pltpu.VMEM((tm, d), BF16)],
        compiler_params=pltpu.CompilerParams(
            dimension_semantics=("parallel", "arbitrary"), vmem_limit_bytes=VMEM_LIMIT_BYTES),
        name="inproj",
    )(x2, g, w, wfg, *tables)


def _fgate_cumsum_kernel(fg_ref, b_ref, c_ref, *, blk):
    s_len = fg_ref.shape[1]
    row = lax.broadcasted_iota(jnp.int32, (blk, blk), 0)
    col = lax.broadcasted_iota(jnp.int32, (blk, blk), 1)
    tri = jnp.where(row >= col, 1.0, 0.0).astype(BF16)
    carry = jnp.zeros((1, LANES), F32)
    for n in range(s_len // blk):
        x = fg_ref[0, n * blk:(n + 1) * blk, :] + b_ref[...]
        ls = jnp.minimum(x, 0.0) - jnp.log1p(jnp.exp(-jnp.abs(x)))
        hi, mid, lo = _split3_bf16(ls)
        cb = _dot(tri, hi) + _dot(tri, mid) + _dot(tri, lo) + carry
        c_ref[0, n * blk:(n + 1) * blk, :] = cb
        carry = cb[blk - 1:blk, :]


def _fgate_cumsum(fg3, bias):
    b, s_len, _ = fg3.shape
    blk = min(256, s_len)
    return pl.pallas_call(
        functools.partial(_fgate_cumsum_kernel, blk=blk),
        grid=(b,),
        in_specs=[pl.BlockSpec((1, s_len, LANES), lambda i: (i, 0, 0)),
                  pl.BlockSpec((1, LANES), lambda i: (0, 0))],
        out_specs=pl.BlockSpec((1, s_len, LANES), lambda i: (i, 0, 0)),
        out_shape=jax.ShapeDtypeStruct((b, s_len, LANES), F32),
        compiler_params=pltpu.CompilerParams(dimension_semantics=("parallel",)),
        name="fgate_cumsum",
    )(fg3, bias)


def _causal_mask(t, strict=False):
    row = lax.broadcasted_iota(jnp.int32, (t, t), 0)
    col = lax.broadcasted_iota(jnp.int32, (t, t), 1)
    return (col < row) if strict else (col <= row)


def _softmax_av(parts):
    m = None
    for s, _ in parts:
        mj = jnp.max(s, axis=-1, keepdims=True)
        m = mj if m is None else jnp.maximum(m, mj)
    acc = None
    l = None
    for s, v in parts:
        p = jnp.exp2(s - m)
        lj = jnp.sum(p, axis=-1, keepdims=True)
        aj = _dot(p.astype(BF16), v)
        l = lj if l is None else l + lj
        acc = aj if acc is None else acc + aj
    return acc / l


def _fox_kernel(q_ref, k_ref, v_ref, ct_ref, o_ref, *, t):
    h = pl.program_id(1)
    s_len = q_ref.shape[1]
    ck = ct_ref[0, pl.ds(h, 1), :] * LOG2E
    diag = _causal_mask(t)
    for i in range(s_len // t):
        lo, hi = i * t, (i + 1) * t
        q = q_ref[0, lo:hi, :]
        bias = ck[:, lo:lo + 1] - ck[:, :hi]
        parts = []
        if i > 0:
            parts.append((_nt_dot(q, k_ref[0, :lo, :]) + bias[:, :lo], v_ref[0, :lo, :]))
        sd = _nt_dot(q, k_ref[0, lo:hi, :]) + bias[:, lo:hi]
        parts.append((jnp.where(diag, sd, NEG_INF), v_ref[0, lo:hi, :]))
        o_ref[0, lo:hi, :] = _softmax_av(parts).astype(o_ref.dtype)


def _moba_kernel(q_ref, k_ref, v_ref, o_ref, *, t):
    s_len = q_ref.shape[1]
    n_blocks = s_len // t

    brow = lax.broadcasted_iota(jnp.int32, (LANES, s_len), 0)
    bcol = lax.broadcasted_iota(jnp.int32, (LANES, s_len), 1)
    member = jnp.where(bcol // t == brow, 1.0, 0.0).astype(BF16)
    k_mean = _dot(member, k_ref[0]) * (1.0 / t)
    km_hi, km_mid, km_lo = _split3_bf16(k_mean)
    lane = lax.broadcasted_iota(jnp.int32, (t, LANES), 1)
    diag = _causal_mask(t)

    for i in range(n_blocks):
        lo, hi = i * t, (i + 1) * t
        q = q_ref[0, lo:hi, :]
        parts = []
        if i > 0:
            s_past = _nt_dot(q, k_ref[0, :lo, :])
            if i <= MOBA_TOPK:
                parts.append((s_past, v_ref[0, :lo, :]))
            else:
                gate = _nt_dot(q, km_hi) + _nt_dot(q, km_mid) + _nt_dot(q, km_lo)
                gate = jnp.where(lane < i, gate, NEG_INF)
                beaten = jnp.zeros((t, LANES), F32)
                for mblk in range(i):
                    gm = gate[:, mblk:mblk + 1]
                    wins = jnp.where(gm > gate, 1.0, jnp.where((gm == gate) & (mblk < lane), 1.0, 0.0))
                    beaten = beaten + wins
                for n in range(i):
                    keep_n = beaten[:, n:n + 1] < MOBA_TOPK
                    s_n = jnp.where(keep_n, s_past[:, n * t:(n + 1) * t], NEG_INF)
                    parts.append((s_n, v_ref[0, n * t:(n + 1) * t, :]))
        sd = _nt_dot(q, k_ref[0, lo:hi, :])
        parts.append((jnp.where(diag, sd, NEG_INF), v_ref[0, lo:hi, :]))
        o_ref[0, lo:hi, :] = _softmax_av(parts).astype(o_ref.dtype)


def _diff_kernel(lam_init_ref, q_ref, k_ref, v_ref, lq1_ref, lk1_ref, lq2_ref, lk2_ref, gsub_ref, o_ref,
                 *, t):
    s_len = q_ref.shape[1]
    lam_init = lam_init_ref[0]
    lam = (jnp.exp(jnp.sum(lq1_ref[...] * lk1_ref[...], axis=-1, keepdims=True))
           - jnp.exp(jnp.sum(lq2_ref[...] * lk2_ref[...], axis=-1, keepdims=True)) + lam_init)
    lane = lax.broadcasted_iota(jnp.int32, (t, LANES), 1)
    diag = _causal_mask(t)
    for i in range(s_len // t):
        lo, hi = i * t, (i + 1) * t
        q = q_ref[0, lo:hi, :]
        zero = jnp.zeros_like(q)
        halves = (jnp.where(lane < DIFF_QK_DIM, q, zero), jnp.where(lane >= DIFF_QK_DIM, q, zero))
        outs = []
        for qh in halves:
            parts = []
            if i > 0:
                parts.append((_nt_dot(qh, k_ref[0, :lo, :]), v_ref[0, :lo, :]))
            sd = _nt_dot(qh, k_ref[0, lo:hi, :])
            parts.append((jnp.where(diag, sd, NEG_INF), v_ref[0, lo:hi, :]))
            outs.append(_softmax_av(parts))
        out = outs[0] - lam * outs[1]
        ms = jnp.mean(out * out, axis=-1, keepdims=True)
        y = out * lax.rsqrt(ms + DIFF_SUBLN_EPS) * gsub_ref[...]
        o_ref[0, lo:hi, :] = (y * (1.0 - lam_init)).astype(o_ref.dtype)


def _sb_kernel(q_ref, k_ref, v_ref, o_ref, *, t):
    s_len = q_ref.shape[1]
    row = lax.broadcasted_iota(jnp.int32, (t, t), 0)
    col = lax.broadcasted_iota(jnp.int32, (t, t), 1)
    later = jnp.where(row > col, 1.0, 0.0).astype(BF16)
    strict = _causal_mask(t, strict=True)
    for i in range(s_len // t):
        lo, hi = i * t, (i + 1) * t
        q = q_ref[0, lo:hi, :]
        z_all = _nt_dot(q, k_ref[0, :hi, :])
        run = None
        acc = None
        for n in range(i, -1, -1):
            z = z_all[:, n * t:(n + 1) * t]
            sp = jnp.log2(1.0 + jnp.exp2(-jnp.abs(z)))
            log_beta = jnp.minimum(z, 0.0) - sp
            log_1m = log_beta - z
            if n == i:
                log_1m = jnp.where(strict, log_1m, 0.0)
            l_hi, l_lo = _split2_bf16(log_1m)
            between = _dot(l_hi, later) + _dot(l_lo, later)
            if run is not None:
                between = between + run
            a = jnp.exp2(log_beta + between)
            if n == i:
                a = jnp.where(strict, a, 0.0)
            an = _dot(a.astype(BF16), v_ref[0, n * t:(n + 1) * t, :])
            acc = an if acc is None else acc + an
            rs = jnp.sum(log_1m, axis=-1, keepdims=True)
            run = rs if run is None else run + rs
        o_ref[0, lo:hi, :] = acc.astype(o_ref.dtype)


def _attention_call(kernel_fn, name, z3, tile_q, extra_inputs=(), extra_specs=(), prefix_inputs=(),
                    prefix_specs=()):
    b, s_len, _ = z3.shape
    t = min(ATTN_TILE, s_len)
    hq, hk, hv = (tile_q * HEADS_PER_GROUP, (tile_q + 1) * HEADS_PER_GROUP, (tile_q + 2) * HEADS_PER_GROUP)
    seq_spec = lambda h0: pl.BlockSpec((1, s_len, HEAD_DIM), lambda bi, h: (bi, 0, h0 + h))
    return pl.pallas_call(
        functools.partial(kernel_fn, t=t),
        grid=(b, HEADS_PER_GROUP),
        in_specs=[*prefix_specs, seq_spec(hq), seq_spec(hk), seq_spec(hv), *extra_specs],
        out_specs=seq_spec(0),
        out_shape=jax.ShapeDtypeStruct((b, s_len, GROUP_WIDTH), BF16),
        compiler_params=pltpu.CompilerParams(
            dimension_semantics=("parallel", "parallel"), vmem_limit_bytes=VMEM_LIMIT_BYTES),
        name=name,
    )(*prefix_inputs, z3, z3, z3, *extra_inputs)


def _fox_attention(z3, c_t):
    s_len = z3.shape[1]
    return _attention_call(_fox_kernel, "fox", z3, T_FQ, (c_t,),
                           (pl.BlockSpec((1, SUBLANES, s_len), lambda bi, h: (bi, 0, 0)),))


def _moba_attention(z3):
    assert ATTN_TILE == MOBA_BLOCK and z3.shape[1] % MOBA_BLOCK == 0 and z3.shape[1] // MOBA_BLOCK <= LANES
    return _attention_call(_moba_kernel, "moba", z3, T_MQ)


def _diff_attention(z3, lq1, lk1, lq2, lk2, gsub, lam_init):
    vec = lambda n: pl.BlockSpec((1, n), lambda bi, h: (0, 0))
    return _attention_call(
        _diff_kernel, "diff", z3, T_DQ, (lq1, lk1, lq2, lk2, gsub),
        (vec(DIFF_QK_DIM), vec(DIFF_QK_DIM), vec(DIFF_QK_DIM), vec(DIFF_QK_DIM), vec(HEAD_DIM)),
        prefix_inputs=(jnp.full((1,), lam_init, F32),),
        prefix_specs=(pl.BlockSpec(memory_space=pltpu.SMEM),))


def _sb_attention(z3):
    return _attention_call(_sb_kernel, "stickbreak", z3, T_SQ)


def _outproj_kernel(a0_ref, a1_ref, a2_ref, a3_ref, w_ref, x_ref, o_ref):
    acc = x_ref[...]
    for g, a_ref in enumerate((a0_ref, a1_ref, a2_ref, a3_ref)):
        acc = acc + _dot(a_ref[...], w_ref[g * GROUP_WIDTH:(g + 1) * GROUP_WIDTH, :])
    o_ref[...] = acc


def _outproj(mixed, w, x2, tm=1024, tn=512):
    m, d = x2.shape
    tm = min(tm, m)
    a_spec = pl.BlockSpec((tm, GROUP_WIDTH), lambda i, j: (i, 0))
    return pl.pallas_call(
        _outproj_kernel,
        grid=(m // tm, d // tn),
        in_specs=[a_spec, a_spec, a_spec, a_spec,
                  pl.BlockSpec((w.shape[0], tn), lambda i, j: (0, j)),
                  pl.BlockSpec((tm, tn), lambda i, j: (i, j))],
        out_specs=pl.BlockSpec((tm, tn), lambda i, j: (i, j)),
        out_shape=jax.ShapeDtypeStruct((m, d), F32),
        compiler_params=pltpu.CompilerParams(
            dimension_semantics=("parallel", "parallel"), vmem_limit_bytes=VMEM_LIMIT_BYTES),
        name="outproj",
    )(*mixed, w, x2)


def _ffn_up_kernel(x_ref, g_ref, wg_ref, wu_ref, o_ref, h_scr):
    @pl.when(pl.program_id(1) == 0)
    def _():
        x = x_ref[...]
        ms = jnp.mean(x * x, axis=-1, keepdims=True)
        h_scr[...] = (x * lax.rsqrt(ms + NORM_EPS) * g_ref[...]).astype(BF16)

    h = h_scr[...]
    gate = _dot(h, wg_ref[...])
    up = _dot(h, wu_ref[...])
    o_ref[...] = (gate / (1.0 + jnp.exp(-gate)) * up).astype(BF16)


def _ffn_up(x2, g, wg, wu, tm=1024, tn=512):
    m, d = x2.shape
    f = wg.shape[1]
    tm = min(tm, m)
    return pl.pallas_call(
        _ffn_up_kernel,
        grid=(m // tm, f // tn),
        in_specs=[pl.BlockSpec((tm, d), lambda i, j: (i, 0)),
                  pl.BlockSpec((1, d), lambda i, j: (0, 0)),
                  pl.BlockSpec((d, tn), lambda i, j: (0, j)),
                  pl.BlockSpec((d, tn), lambda i, j: (0, j))],
        out_specs=pl.BlockSpec((tm, tn), lambda i, j: (i, j)),
        out_shape=jax.ShapeDtypeStruct((m, f), BF16),
        scratch_shapes=[pltpu.VMEM((tm, d), BF16)],
        compiler_params=pltpu.CompilerParams(
            dimension_semantics=("parallel", "arbitrary"), vmem_limit_bytes=VMEM_LIMIT_BYTES),
        name="ffn_up",
    )(x2, g, wg, wu)


def _ffn_down_kernel(a_ref, w_ref, x_ref, o_ref):
    o_ref[...] = x_ref[...] + _dot(a_ref[...], w_ref[...])


def _ffn_down(act, w, x2, tm=512, tn=512):
    m, d = x2.shape
    f = act.shape[1]
    tm = min(tm, m)
    return pl.pallas_call(
        _ffn_down_kernel,
        grid=(m // tm, d // tn),
        in_specs=[pl.BlockSpec((tm, f), lambda i, j: (i, 0)),
                  pl.BlockSpec((f, tn), lambda i, j: (0, j)),
                  pl.BlockSpec((tm, tn), lambda i, j: (i, j))],
        out_specs=pl.BlockSpec((tm, tn), lambda i, j: (i, j)),
        out_shape=jax.ShapeDtypeStruct((m, d), F32),
        compiler_params=pltpu.CompilerParams(
            dimension_semantics=("parallel", "parallel"), vmem_limit_bytes=VMEM_LIMIT_BYTES),
        name="ffn_down",
    )(act, w, x2)


def _final_norm_kernel(x_ref, g_ref, o_ref):
    x = x_ref[...]
    ms = jnp.mean(x * x, axis=-1, keepdims=True)
    o_ref[...] = x * lax.rsqrt(ms + NORM_EPS) * g_ref[...]


def _final_norm(x2, g, tm=512):
    m, d = x2.shape
    tm = min(tm, m)
    return pl.pallas_call(
        _final_norm_kernel,
        grid=(m // tm,),
        in_specs=[pl.BlockSpec((tm, d), lambda i: (i, 0)), pl.BlockSpec((1, d), lambda i: (0, 0))],
        out_specs=pl.BlockSpec((tm, d), lambda i: (i, 0)),
        out_shape=jax.ShapeDtypeStruct((m, d), F32),
        compiler_params=pltpu.CompilerParams(dimension_semantics=("parallel",)),
        name="final_norm",
    )(x2, g)


def _rope_tables(s_len):
    def cos_sin(dim):
        inv = 1.0 / (ROPE_THETA ** (jnp.arange(0, dim, 2, dtype=F32) / dim))
        ang = jnp.arange(s_len, dtype=F32)[:, None] * inv[None, :]
        return jnp.cos(ang), jnp.sin(ang)

    cf, sf = cos_sin(HEAD_DIM)
    ch, sh = cos_sin(DIFF_QK_DIM)
    zh = jnp.zeros_like(sh)
    return (jnp.concatenate([cf, cf], axis=-1),
            jnp.concatenate([-sf, sf], axis=-1),
            jnp.concatenate([ch, ch, ch, ch], axis=-1),
            jnp.concatenate([-sh, zh, -sh, zh], axis=-1),
            jnp.concatenate([zh, sh, zh, sh], axis=-1))


def _reorder_w_in(w_in):
    gw = GROUP_WIDTH
    fg0 = 3 * gw
    w_main = jnp.concatenate([w_in[..., :fg0], w_in[..., fg0 + HEADS_PER_GROUP:]], axis=-1)
    w_fg = jnp.pad(w_in[..., fg0:fg0 + HEADS_PER_GROUP], ((0, 0), (0, 0), (0, LANES - HEADS_PER_GROUP)))
    return w_main.astype(BF16), w_fg.astype(BF16)


def kernel(x, w_in, b_fgate, w_out, diff_lq1, diff_lk1, diff_lq2, diff_lk2, diff_subln, attn_norm,
           w_gate, w_up, w_down, ffn_norm, final_norm):
    b, s_len, d = x.shape
    depth = w_in.shape[0]
    m = b * s_len
    tables = _rope_tables(s_len)
    w_main, w_fg = _reorder_w_in(w_in)
    w_out_b = w_out.astype(BF16)
    w_gate_b = w_gate.astype(BF16)
    w_up_b = w_up.astype(BF16)
    w_down_b = w_down.astype(BF16)
    b_fg = jnp.pad(b_fgate, ((0, 0), (0, LANES - HEADS_PER_GROUP)))

    x2 = x.reshape(m, d)
    for l in range(depth):
        z, fg = _inproj(x2, attn_norm[l][None, :], w_main[l], w_fg[l], tables, s_len, tm=1024)
        z3 = z.reshape(b, s_len, N_PROJ * GROUP_WIDTH)
        c = _fgate_cumsum(fg.reshape(b, s_len, LANES), b_fg[l][None, :])
        c_t = jnp.swapaxes(c[:, :, :SUBLANES], 1, 2)
        lam_init = 0.8 - 0.6 * math.exp(-0.3 * l)
        fox = _fox_attention(z3, c_t)
        moba = _moba_attention(z3)
        diff = _diff_attention(z3, diff_lq1[l][None, :], diff_lk1[l][None, :], diff_lq2[l][None, :],
                               diff_lk2[l][None, :], diff_subln[l][None, :], lam_init)
        sb = _sb_attention(z3)
        mixed = [a.reshape(m, GROUP_WIDTH) for a in (fox, moba, diff, sb)]
        x2 = _outproj(mixed, w_out_b[l], x2)
        act = _ffn_up(x2, ffn_norm[l][None, :], w_gate_b[l], w_up_b[l])
        x2 = _ffn_down(act, w_down_b[l], x2)
    return _final_norm(x2, final_norm[None, :]).reshape(b, s_len, d)
```

```python
import functools
import math

import jax
import jax.numpy as jnp
from jax import lax
from jax.experimental import pallas as pl
from jax.experimental.pallas import tpu as pltpu

F32 = jnp.float32
BF16 = jnp.bfloat16

HEAD_DIM = 128
HEADS_PER_GROUP = 4
GROUP_WIDTH = HEADS_PER_GROUP * HEAD_DIM
N_PROJ = 12
GATE_TILE = 3
DIFF_QK_DIM = HEAD_DIM // 2
MOBA_BLOCK = 256
MOBA_TOPK = 3
ROPE_THETA = 10000.0
NORM_EPS = 1e-6
DIFF_SUBLN_EPS = 1e-5
LANES = 128
SUBLANES = 8
NEG_INF = float("-inf")
LOG2E = math.log2(math.e)
ATTN_TILE = 256
NORM_ROWS = 512
MM_ROWS = 1024
DOWN_ROWS = 512

VMEM_LIMIT_BYTES = 56 * 1024 * 1024

(T_FQ, T_FK, T_FV, T_MQ, T_MK, T_MV, T_DQ, T_DK, T_DV, T_SQ, T_SK, T_SV) = range(N_PROJ)


def _nt_dot(a, b):
    return lax.dot_general(a, b, (((1,), (1,)), ((), ())), preferred_element_type=F32)


def _dot(a, b):
    return jnp.dot(a, b, preferred_element_type=F32)


def _split2_bf16(x):
    hi = x.astype(BF16)
    lo = (x - hi.astype(F32)).astype(BF16)
    return hi, lo


def _split3_bf16(x):
    hi = x.astype(BF16)
    r = x - hi.astype(F32)
    mid = r.astype(BF16)
    lo = (r - mid.astype(F32)).astype(BF16)
    return hi, mid, lo


def _rms_normalize(x, g, eps):
    ms = jnp.mean(x * x, axis=-1, keepdims=True)
    return x * lax.rsqrt(ms + eps) * g


def _params(*semantics):
    return pltpu.CompilerParams(dimension_semantics=semantics, vmem_limit_bytes=VMEM_LIMIT_BYTES)


def _norm_kernel(x_ref, g_ref, o_ref):
    o_ref[...] = _rms_normalize(x_ref[...], g_ref[...], NORM_EPS).astype(o_ref.dtype)


def _norm(x2, g, out_dtype):
    m, d = x2.shape
    tm = min(NORM_ROWS, m)
    return pl.pallas_call(
        _norm_kernel,
        grid=(m // tm,),
        in_specs=[pl.BlockSpec((tm, d), lambda i: (i, 0)), pl.BlockSpec((1, d), lambda i: (0, 0))],
        out_specs=pl.BlockSpec((tm, d), lambda i: (i, 0)),
        out_shape=jax.ShapeDtypeStruct((m, d), out_dtype),
        compiler_params=_params("arbitrary"),
        name="norm",
    )(x2, g)


def _norm_gate_kernel(x_ref, g_ref, wfg_ref, b_ref, h_ref, ct_ref, carry_ref):
    tm = x_ref.shape[1]

    @pl.when(pl.program_id(1) == 0)
    def _():
        carry_ref[...] = jnp.zeros_like(carry_ref)

    h = _rms_normalize(x_ref[0], g_ref[...], NORM_EPS).astype(BF16)
    h_ref[0] = h
    f = _dot(h, wfg_ref[...].astype(BF16)) + b_ref[...]
    ls = jnp.minimum(f, 0.0) - jnp.log1p(jnp.exp(-jnp.abs(f)))
    row = lax.broadcasted_iota(jnp.int32, (tm, tm), 0)
    col = lax.broadcasted_iota(jnp.int32, (tm, tm), 1)
    tri = jnp.where(row >= col, 1.0, 0.0).astype(BF16)
    hi, mid, lo = _split3_bf16(ls)
    c = _dot(tri, hi) + _dot(tri, mid) + _dot(tri, lo) + carry_ref[...]
    carry_ref[...] = c[tm - 1:tm, :]
    ct_ref[0] = jnp.transpose(c)[:SUBLANES, :]


def _norm_gate(x3, g, w_in, layer, b_fg):
    b, s_len, d = x3.shape
    tm = min(NORM_ROWS, s_len)
    gate_block = GATE_TILE * GROUP_WIDTH // LANES
    return pl.pallas_call(
        _norm_gate_kernel,
        grid=(b, s_len // tm),
        in_specs=[pl.BlockSpec((1, tm, d), lambda bi, i: (bi, i, 0)),
                  pl.BlockSpec((1, d), lambda bi, i: (0, 0)),
                  pl.BlockSpec((None, d, LANES), lambda bi, i: (layer, 0, gate_block)),
                  pl.BlockSpec((1, LANES), lambda bi, i: (0, 0))],
        out_specs=[pl.BlockSpec((1, tm, d), lambda bi, i: (bi, i, 0)),
                   pl.BlockSpec((1, SUBLANES, tm), lambda bi, i: (bi, 0, i))],
        out_shape=[jax.ShapeDtypeStruct((b, s_len, d), BF16),
                   jax.ShapeDtypeStruct((b, SUBLANES, s_len), F32)],
        scratch_shapes=[pltpu.VMEM((1, LANES), F32)],
        compiler_params=_params("arbitrary", "arbitrary"),
        name="norm_gate",
    )(x3, g, w_in, b_fg)


def _inproj_kernel(h_ref, wa_ref, wb_ref, cf_ref, sf_ref, ch_ref, sha_ref, shb_ref, z_ref, w_scr, *, npos):
    j = pl.program_id(0)
    i = pl.program_id(1)
    tm = h_ref.shape[0]
    shift = HEADS_PER_GROUP

    def slabs():
        for s in range(HEADS_PER_GROUP):
            yield s, slice(s * HEAD_DIM, (s + 1) * HEAD_DIM)

    @pl.when((i == 0) & (j < GATE_TILE))
    def _():
        w_scr[...] = wa_ref[...].astype(BF16)

    @pl.when((i == 0) & (j >= GATE_TILE))
    def _():
        lane = lax.broadcasted_iota(jnp.int32, (wa_ref.shape[0], LANES), 1)
        for s, sl in slabs():
            cur = wa_ref[:, sl]
            nxt = wa_ref[:, (s + 1) * HEAD_DIM:(s + 2) * HEAD_DIM] if s + 1 < HEADS_PER_GROUP else wb_ref[...]
            moved = jnp.where(lane < LANES - shift, pltpu.roll(cur, LANES - shift, 1),
                              pltpu.roll(nxt, LANES - shift, 1))
            w_scr[:, sl] = moved.astype(BF16)

    acc = _dot(h_ref[...], w_scr[...])
    scale_full = HEAD_DIM ** -0.5 * LOG2E
    scale_half = DIFF_QK_DIM ** -0.5 * LOG2E
    pos = pl.ds(pl.multiple_of((i % npos) * tm, tm), tm)

    def rope_full(a):
        return a * cf_ref[pos, :] + pltpu.roll(a, HEAD_DIM // 2, 1) * sf_ref[pos, :]

    def rope_half(a):
        return (a * ch_ref[pos, :] + pltpu.roll(a, HEAD_DIM - DIFF_QK_DIM // 2, 1) * sha_ref[pos, :]
                + pltpu.roll(a, DIFF_QK_DIM // 2, 1) * shb_ref[pos, :])

    def emit(fn):
        for s, sl in slabs():
            z_ref[0, s] = fn(acc[:, sl]).astype(BF16)

    @pl.when((j == T_FQ) | (j == T_SQ))
    def _():
        emit(lambda a: a * scale_full)

    @pl.when(j == T_MQ)
    def _():
        emit(lambda a: rope_full(a) * scale_full)

    @pl.when(j == T_MK)
    def _():
        emit(rope_full)

    @pl.when(j == T_DQ)
    def _():
        emit(lambda a: rope_half(a) * scale_half)

    @pl.when(j == T_DK)
    def _():
        emit(rope_half)

    @pl.when((j == T_FK) | (j == T_FV) | (j == T_MV) | (j == T_DV) | (j == T_SK) | (j == T_SV))
    def _():
        emit(lambda a: a)


def _inproj(h2, w_in, layer, tables, b, s_len):
    m, d = h2.shape
    tm = min(MM_ROWS, s_len)
    npos = s_len // tm
    tab_spec = pl.BlockSpec((s_len, LANES), lambda j, i: (0, 0))
    return pl.pallas_call(
        functools.partial(_inproj_kernel, npos=npos),
        grid=(N_PROJ, m // tm),
        in_specs=[
            pl.BlockSpec((tm, d), lambda j, i: (i, 0)),
            pl.BlockSpec((None, d, GROUP_WIDTH), lambda j, i: (layer, 0, j)),
            pl.BlockSpec((None, d, LANES), lambda j, i: (layer, 0, (j + 1) * HEADS_PER_GROUP)),
            tab_spec, tab_spec, tab_spec, tab_spec, tab_spec,
        ],
        out_specs=pl.BlockSpec((1, HEADS_PER_GROUP, tm, HEAD_DIM), lambda j, i: (i // npos, j, i % npos, 0)),
        out_shape=jax.ShapeDtypeStruct((b, N_PROJ * HEADS_PER_GROUP, s_len, HEAD_DIM), BF16),
        scratch_shapes=[pltpu.VMEM((d, GROUP_WIDTH), BF16)],
        compiler_params=_params("arbitrary", "arbitrary"),
        name="inproj",
    )(h2, w_in, w_in, *tables)


def _causal_mask(t, strict=False):
    row = lax.broadcasted_iota(jnp.int32, (t, t), 0)
    col = lax.broadcasted_iota(jnp.int32, (t, t), 1)
    return (col < row) if strict else (col <= row)


def _softmax_weights(logits):
    m = None
    for s in logits:
        mj = jnp.max(s, axis=-1, keepdims=True)
        m = mj if m is None else jnp.maximum(m, mj)
    ps = [jnp.exp2(s - m) for s in logits]
    l = None
    for p in ps:
        lj = jnp.sum(p, axis=-1, keepdims=True)
        l = lj if l is None else l + lj
    return ps, l


def _weighted_values(ps, vs):
    acc = None
    for p, v in zip(ps, vs):
        aj = _dot(p.astype(BF16), v)
        acc = aj if acc is None else acc + aj
    return acc


def _fox_kernel(q_ref, k_ref, v_ref, ct_ref, o_ref, *, t):
    h = pl.program_id(1)
    s_len = q_ref.shape[2]
    ck = ct_ref[0, pl.ds(h, 1), :] * LOG2E
    diag = _causal_mask(t)
    for i in range(s_len // t):
        lo, hi = i * t, (i + 1) * t
        q = q_ref[0, 0, lo:hi, :]
        bias = ck[:, lo:lo + 1] - ck[:, :hi]
        logits, vs = [], []
        if i > 0:
            logits.append(_nt_dot(q, k_ref[0, 0, :lo, :]) + bias[:, :lo])
            vs.append(v_ref[0, 0, :lo, :])
        sd = _nt_dot(q, k_ref[0, 0, lo:hi, :]) + bias[:, lo:hi]
        logits.append(jnp.where(diag, sd, NEG_INF))
        vs.append(v_ref[0, 0, lo:hi, :])
        ps, l = _softmax_weights(logits)
        o_ref[0, 0, lo:hi, :] = (_weighted_values(ps, vs) / l).astype(o_ref.dtype)


def _moba_kernel(q_ref, k_ref, v_ref, o_ref, *, t):
    s_len = q_ref.shape[2]
    n_blocks = s_len // t

    brow = lax.broadcasted_iota(jnp.int32, (LANES, s_len), 0)
    bcol = lax.broadcasted_iota(jnp.int32, (LANES, s_len), 1)
    member = jnp.where(bcol // t == brow, 1.0, 0.0).astype(BF16)
    k_mean = _dot(member, k_ref[0, 0]) * (1.0 / t)
    km_hi, km_mid, km_lo = _split3_bf16(k_mean)
    lane = lax.broadcasted_iota(jnp.int32, (t, LANES), 1)
    diag = _causal_mask(t)

    for i in range(n_blocks):
        lo, hi = i * t, (i + 1) * t
        q = q_ref[0, 0, lo:hi, :]
        logits, vs = [], []
        if i > 0:
            s_past = _nt_dot(q, k_ref[0, 0, :lo, :])
            if i <= MOBA_TOPK:
                logits.append(s_past)
                vs.append(v_ref[0, 0, :lo, :])
            else:
                gate = _nt_dot(q, km_hi) + _nt_dot(q, km_mid) + _nt_dot(q, km_lo)
                gate = jnp.where(lane < i, gate, NEG_INF)
                beaten = jnp.zeros((t, LANES), F32)
                for mblk in range(i):
                    gm = gate[:, mblk:mblk + 1]
                    wins = jnp.where(gm > gate, 1.0, jnp.where((gm == gate) & (mblk < lane), 1.0, 0.0))
                    beaten = beaten + wins
                for n in range(i):
                    keep_n = beaten[:, n:n + 1] < MOBA_TOPK
                    logits.append(jnp.where(keep_n, s_past[:, n * t:(n + 1) * t], NEG_INF))
                    vs.append(v_ref[0, 0, n * t:(n + 1) * t, :])
        sd = _nt_dot(q, k_ref[0, 0, lo:hi, :])
        logits.append(jnp.where(diag, sd, NEG_INF))
        vs.append(v_ref[0, 0, lo:hi, :])
        ps, l = _softmax_weights(logits)
        o_ref[0, 0, lo:hi, :] = (_weighted_values(ps, vs) / l).astype(o_ref.dtype)


def _diff_kernel(lam_init_ref, q_ref, k_ref, v_ref, lq1_ref, lk1_ref, lq2_ref, lk2_ref, gsub_ref, o_ref,
                 *, t):
    s_len = q_ref.shape[2]
    lam_init = lam_init_ref[0]
    lam = (jnp.exp(jnp.sum(lq1_ref[...] * lk1_ref[...], axis=-1, keepdims=True))
           - jnp.exp(jnp.sum(lq2_ref[...] * lk2_ref[...], axis=-1, keepdims=True)) + lam_init)
    lane = lax.broadcasted_iota(jnp.int32, (t, LANES), 1)
    diag = _causal_mask(t)
    for i in range(s_len // t):
        lo, hi = i * t, (i + 1) * t
        q = q_ref[0, 0, lo:hi, :]
        zero = jnp.zeros_like(q)
        halves = (jnp.where(lane < DIFF_QK_DIM, q, zero), jnp.where(lane >= DIFF_QK_DIM, q, zero))
        weights = []
        for qh in halves:
            logits = []
            if i > 0:
                logits.append(_nt_dot(qh, k_ref[0, 0, :lo, :]))
            logits.append(jnp.where(diag, _nt_dot(qh, k_ref[0, 0, lo:hi, :]), NEG_INF))
            weights.append(_softmax_weights(logits))
        (ps1, l1), (ps2, l2) = weights
        r1 = 1.0 / l1
        r2 = lam / l2
        ps = [p1 * r1 - p2 * r2 for p1, p2 in zip(ps1, ps2)]
        vs = ([v_ref[0, 0, :lo, :]] if i > 0 else []) + [v_ref[0, 0, lo:hi, :]]
        out = _weighted_values(ps, vs)
        y = _rms_normalize(out, gsub_ref[...], DIFF_SUBLN_EPS)
        o_ref[0, 0, lo:hi, :] = (y * (1.0 - lam_init)).astype(o_ref.dtype)


def _sb_kernel(q_ref, k_ref, v_ref, o_ref, *, t):
    s_len = q_ref.shape[2]
    row = lax.broadcasted_iota(jnp.int32, (t, t), 0)
    col = lax.broadcasted_iota(jnp.int32, (t, t), 1)
    later = jnp.where(row > col, 1.0, 0.0).astype(BF16)
    strict = _causal_mask(t, strict=True)
    for i in range(s_len // t):
        lo, hi = i * t, (i + 1) * t
        q = q_ref[0, 0, lo:hi, :]
        z_all = _nt_dot(q, k_ref[0, 0, :hi, :])
        run = None
        acc = None
        for n in range(i, -1, -1):
            z = z_all[:, n * t:(n + 1) * t]
            sp = jnp.log2(1.0 + jnp.exp2(-jnp.abs(z)))
            log_beta = jnp.minimum(z, 0.0) - sp
            log_1m = log_beta - z
            if n == i:
                log_1m = jnp.where(strict, log_1m, 0.0)
            l_hi, l_lo = _split2_bf16(log_1m)
            between = _dot(l_hi, later) + _dot(l_lo, later)
            if run is not None:
                between = between + run
            a = jnp.exp2(log_beta + between)
            if n == i:
                a = jnp.where(strict, a, 0.0)
            an = _dot(a.astype(BF16), v_ref[0, 0, n * t:(n + 1) * t, :])
            acc = an if acc is None else acc + an
            rs = jnp.sum(log_1m, axis=-1, keepdims=True)
            run = rs if run is None else run + rs
        o_ref[0, 0, lo:hi, :] = acc.astype(o_ref.dtype)


def _attention_call(kernel_fn, name, z4, tile_q, extra_inputs=(), extra_specs=(), prefix_inputs=(),
                    prefix_specs=()):
    b, _, s_len, _ = z4.shape
    t = min(ATTN_TILE, s_len)
    hq, hk, hv = (tile_q * HEADS_PER_GROUP, (tile_q + 1) * HEADS_PER_GROUP, (tile_q + 2) * HEADS_PER_GROUP)
    seq_spec = lambda h0: pl.BlockSpec((1, 1, s_len, HEAD_DIM), lambda bi, h: (bi, h0 + h, 0, 0))
    return pl.pallas_call(
        functools.partial(kernel_fn, t=t),
        grid=(b, HEADS_PER_GROUP),
        in_specs=[*prefix_specs, seq_spec(hq), seq_spec(hk), seq_spec(hv), *extra_specs],
        out_specs=seq_spec(0),
        out_shape=jax.ShapeDtypeStruct((b, HEADS_PER_GROUP, s_len, HEAD_DIM), BF16),
        compiler_params=_params("arbitrary", "arbitrary"),
        name=name,
    )(*prefix_inputs, z4, z4, z4, *extra_inputs)


def _fox_attention(z4, c_t):
    s_len = z4.shape[2]
    return _attention_call(_fox_kernel, "fox", z4, T_FQ, (c_t,),
                           (pl.BlockSpec((1, SUBLANES, s_len), lambda bi, h: (bi, 0, 0)),))


def _moba_attention(z4):
    assert ATTN_TILE == MOBA_BLOCK and z4.shape[2] % MOBA_BLOCK == 0 and z4.shape[2] // MOBA_BLOCK <= LANES
    return _attention_call(_moba_kernel, "moba", z4, T_MQ)


def _diff_attention(z4, lq1, lk1, lq2, lk2, gsub, lam_init):
    vec = lambda n: pl.BlockSpec((1, n), lambda bi, h: (0, 0))
    return _attention_call(
        _diff_kernel, "diff", z4, T_DQ, (lq1, lk1, lq2, lk2, gsub),
        (vec(DIFF_QK_DIM), vec(DIFF_QK_DIM), vec(DIFF_QK_DIM), vec(DIFF_QK_DIM), vec(HEAD_DIM)),
        prefix_inputs=(jnp.full((1,), lam_init, F32),),
        prefix_specs=(pl.BlockSpec(memory_space=pltpu.SMEM),))


def _sb_attention(z4):
    return _attention_call(_sb_kernel, "stickbreak", z4, T_SQ)


def _cast_weights_once(w_ref, w_scr):
    @pl.when(pl.program_id(1) == 0)
    def _():
        w_scr[...] = w_ref[...].astype(BF16)


def _outproj_kernel(a0_ref, a1_ref, a2_ref, a3_ref, w_ref, x_ref, o_ref, w_scr):
    _cast_weights_once(w_ref, w_scr)
    heads = [a_ref[0, s] for a_ref in (a0_ref, a1_ref, a2_ref, a3_ref) for s in range(HEADS_PER_GROUP)]
    mixed = jnp.concatenate(heads, axis=-1)
    o_ref[...] = x_ref[...] + _dot(mixed, w_scr[...])


def _outproj(mixed, w, layer, x2, s_len, tn=512):
    m, d = x2.shape
    tm = min(MM_ROWS, s_len)
    npos = s_len // tm
    a_spec = pl.BlockSpec((1, HEADS_PER_GROUP, tm, HEAD_DIM), lambda j, i: (i // npos, 0, i % npos, 0))
    return pl.pallas_call(
        _outproj_kernel,
        grid=(d // tn, m // tm),
        in_specs=[a_spec, a_spec, a_spec, a_spec,
                  pl.BlockSpec((None, w.shape[1], tn), lambda j, i: (layer, 0, j)),
                  pl.BlockSpec((tm, tn), lambda j, i: (i, j))],
        out_specs=pl.BlockSpec((tm, tn), lambda j, i: (i, j)),
        out_shape=jax.ShapeDtypeStruct((m, d), F32),
        scratch_shapes=[pltpu.VMEM((w.shape[1], tn), BF16)],
        compiler_params=_params("arbitrary", "arbitrary"),
        name="outproj",
    )(*mixed, w, x2)


def _ffn_up_kernel(h_ref, wg_ref, wu_ref, o_ref, wg_scr, wu_scr):
    _cast_weights_once(wg_ref, wg_scr)
    _cast_weights_once(wu_ref, wu_scr)
    h = h_ref[...]
    gate = _dot(h, wg_scr[...])
    up = _dot(h, wu_scr[...])
    o_ref[...] = (gate / (1.0 + jnp.exp(-gate)) * up).astype(BF16)


def _ffn_up(h2, wg, wu, layer, tn=512):
    m, d = h2.shape
    f = wg.shape[2]
    tm = min(MM_ROWS, m)
    w_spec = pl.BlockSpec((None, d, tn), lambda j, i: (layer, 0, j))
    return pl.pallas_call(
        _ffn_up_kernel,
        grid=(f // tn, m // tm),
        in_specs=[pl.BlockSpec((tm, d), lambda j, i: (i, 0)), w_spec, w_spec],
        out_specs=pl.BlockSpec((tm, tn), lambda j, i: (i, j)),
        out_shape=jax.ShapeDtypeStruct((m, f), BF16),
        scratch_shapes=[pltpu.VMEM((d, tn), BF16), pltpu.VMEM((d, tn), BF16)],
        compiler_params=_params("arbitrary", "arbitrary"),
        name="ffn_up",
    )(h2, wg, wu)


def _ffn_down_kernel(a_ref, w_ref, x_ref, o_ref, w_scr):
    _cast_weights_once(w_ref, w_scr)
    o_ref[...] = x_ref[...] + _dot(a_ref[...], w_scr[...])


def _ffn_down(act, w, layer, x2, tn=512):
    m, d = x2.shape
    f = act.shape[1]
    tm = min(DOWN_ROWS, m)
    return pl.pallas_call(
        _ffn_down_kernel,
        grid=(d // tn, m // tm),
        in_specs=[pl.BlockSpec((tm, f), lambda j, i: (i, 0)),
                  pl.BlockSpec((None, f, tn), lambda j, i: (layer, 0, j)),
                  pl.BlockSpec((tm, tn), lambda j, i: (i, j))],
        out_specs=pl.BlockSpec((tm, tn), lambda j, i: (i, j)),
        out_shape=jax.ShapeDtypeStruct((m, d), F32),
        scratch_shapes=[pltpu.VMEM((f, tn), BF16)],
        compiler_params=_params("arbitrary", "arbitrary"),
        name="ffn_down",
    )(act, w, x2)


def _rope_tables(s_len):
    def cos_sin(dim):
        inv = 1.0 / (ROPE_THETA ** (jnp.arange(0, dim, 2, dtype=F32) / dim))
        ang = jnp.arange(s_len, dtype=F32)[:, None] * inv[None, :]
        return jnp.cos(ang), jnp.sin(ang)

    cf, sf = cos_sin(HEAD_DIM)
    ch, sh = cos_sin(DIFF_QK_DIM)
    zh = jnp.zeros_like(sh)
    return (jnp.concatenate([cf, cf], axis=-1),
            jnp.concatenate([-sf, sf], axis=-1),
            jnp.concatenate([ch, ch, ch, ch], axis=-1),
            jnp.concatenate([-sh, zh, -sh, zh], axis=-1),
            jnp.concatenate([zh, sh, zh, sh], axis=-1))


def kernel(x, w_in, b_fgate, w_out, diff_lq1, diff_lk1, diff_lq2, diff_lk2, diff_subln, attn_norm,
           w_gate, w_up, w_down, ffn_norm, final_norm):
    b, s_len, d = x.shape
    depth = w_in.shape[0]
    m = b * s_len
    tables = _rope_tables(s_len)
    b_fg = jnp.pad(b_fgate, ((0, 0), (0, LANES - HEADS_PER_GROUP)))

    x2 = x.reshape(m, d)
    for l in range(depth):
        h, c_t = _norm_gate(x2.reshape(b, s_len, d), attn_norm[l][None, :], w_in, l, b_fg[l][None, :])
        z4 = _inproj(h.reshape(m, d), w_in, l, tables, b, s_len)
        lam_init = 0.8 - 0.6 * math.exp(-0.3 * l)
        fox = _fox_attention(z4, c_t)
        moba = _moba_attention(z4)
        diff = _diff_attention(z4, diff_lq1[l][None, :], diff_lk1[l][None, :], diff_lq2[l][None, :],
                               diff_lk2[l][None, :], diff_subln[l][None, :], lam_init)
        sb = _sb_attention(z4)
        x2 = _outproj((fox, moba, diff, sb), w_out, l, x2, s_len)
        h2 = _norm(x2, ffn_norm[l][None, :], BF16)
        act = _ffn_up(h2, w_gate, w_up, l)
        x2 = _ffn_down(act, w_down, l, x2)
    return _norm(x2, final_norm[None, :], F32).reshape(b, s_len, d)
```

```python
import functools
import math

import jax
import jax.numpy as jnp
from jax import lax
from jax.experimental import pallas as pl
from jax.experimental.pallas import tpu as pltpu

F32 = jnp.float32
BF16 = jnp.bfloat16

HEAD_DIM = 128
HEADS_PER_GROUP = 4
GROUP_WIDTH = HEADS_PER_GROUP * HEAD_DIM
N_MIXERS = 4
GATE_TILE = 3
DIFF_QK_DIM = HEAD_DIM // 2
MOBA_BLOCK = 256
MOBA_TOPK = 3
ROPE_THETA = 10000.0
NORM_EPS = 1e-6
DIFF_SUBLN_EPS = 1e-5
LANES = 128
SUBLANES = 8
BF16_ROWS = 16
NEG_INF = float("-inf")
LOG2E = math.log2(math.e)
ATTN_TILE = 256
NORM_ROWS = 512
MM_ROWS = 1024
MM_CHUNK = 256
DOWN_ROWS = 512

VMEM_LIMIT_BYTES = 56 * 1024 * 1024

MIXER_FOX, MIXER_MOBA, MIXER_DIFF, MIXER_SB = range(N_MIXERS)
QK_KINDS = ("scale", "plain", "rope_full_scale", "rope_full", "rope_half_scale", "rope_half", "scale", "plain")


def _nt_dot(a, b):
    return lax.dot_general(a, b, (((1,), (1,)), ((), ())), preferred_element_type=F32)


def _dot(a, b):
    return jnp.dot(a, b, preferred_element_type=F32)


def _split2_bf16(x):
    hi = x.astype(BF16)
    lo = (x - hi.astype(F32)).astype(BF16)
    return hi, lo


def _split3_bf16(x):
    hi = x.astype(BF16)
    r = x - hi.astype(F32)
    mid = r.astype(BF16)
    lo = (r - mid.astype(F32)).astype(BF16)
    return hi, mid, lo


def _rms_normalize(x, g, eps):
    ms = jnp.mean(x * x, axis=-1, keepdims=True)
    return x * lax.rsqrt(ms + eps) * g


def _params(*semantics):
    return pltpu.CompilerParams(dimension_semantics=semantics, vmem_limit_bytes=VMEM_LIMIT_BYTES)


def _norm_kernel(x_ref, g_ref, o_ref):
    o_ref[...] = _rms_normalize(x_ref[...], g_ref[...], NORM_EPS).astype(o_ref.dtype)


def _norm(x2, g, out_dtype):
    m, d = x2.shape
    tm = min(NORM_ROWS, m)
    return pl.pallas_call(
        _norm_kernel,
        grid=(m // tm,),
        in_specs=[pl.BlockSpec((tm, d), lambda i: (i, 0)), pl.BlockSpec((1, d), lambda i: (0, 0))],
        out_specs=pl.BlockSpec((tm, d), lambda i: (i, 0)),
        out_shape=jax.ShapeDtypeStruct((m, d), out_dtype),
        compiler_params=_params("arbitrary"),
        name="norm",
    )(x2, g)


def _norm_gate_kernel(x_ref, g_ref, wfg_ref, b_ref, h_ref, cq_ref, ck_ref, carry_ref):
    tm = x_ref.shape[1]

    @pl.when(pl.program_id(1) == 0)
    def _():
        carry_ref[...] = jnp.zeros_like(carry_ref)

    h = _rms_normalize(x_ref[0], g_ref[...], NORM_EPS).astype(BF16)
    h_ref[0] = h
    f = _dot(h, wfg_ref[...].astype(BF16)) + b_ref[...]
    ls = jnp.minimum(f, 0.0) - jnp.log1p(jnp.exp(-jnp.abs(f)))
    row = lax.broadcasted_iota(jnp.int32, (tm, tm), 0)
    col = lax.broadcasted_iota(jnp.int32, (tm, tm), 1)
    tri = jnp.where(row >= col, 1.0, 0.0).astype(BF16)
    hi, mid, lo = _split3_bf16(ls)
    c = _dot(tri, hi) + _dot(tri, mid) + _dot(tri, lo) + carry_ref[...]
    carry_ref[...] = c[tm - 1:tm, :]

    lane = lax.broadcasted_iota(jnp.int32, (tm, LANES), 1)
    for head in range(HEADS_PER_GROUP):
        parts = _split3_bf16(jnp.broadcast_to(c[:, head:head + 1] * LOG2E, (tm, LANES)))
        n_parts = len(parts)
        q_cols = jnp.where(lane < 2 * n_parts, 1.0, 0.0)
        k_cols = jnp.where(lane < n_parts, 1.0, 0.0)
        for n, part in enumerate(parts):
            q_cols = jnp.where(lane == n, part.astype(F32), q_cols)
            k_cols = jnp.where(lane == n_parts + n, -part.astype(F32), k_cols)
        cq_ref[0, head] = q_cols.astype(BF16)
        ck_ref[0, head] = k_cols.astype(BF16)


def _norm_gate(x3, g, w_in, layer, b_fg):
    b, s_len, d = x3.shape
    tm = min(NORM_ROWS, s_len)
    gate_block = GATE_TILE * GROUP_WIDTH // LANES
    aug_spec = pl.BlockSpec((1, HEADS_PER_GROUP, tm, LANES), lambda bi, i: (bi, 0, i, 0))
    aug_shape = jax.ShapeDtypeStruct((b, HEADS_PER_GROUP, s_len, LANES), BF16)
    return pl.pallas_call(
        _norm_gate_kernel,
        grid=(b, s_len // tm),
        in_specs=[pl.BlockSpec((1, tm, d), lambda bi, i: (bi, i, 0)),
                  pl.BlockSpec((1, d), lambda bi, i: (0, 0)),
                  pl.BlockSpec((None, d, LANES), lambda bi, i: (layer, 0, gate_block)),
                  pl.BlockSpec((1, LANES), lambda bi, i: (0, 0))],
        out_specs=[pl.BlockSpec((1, tm, d), lambda bi, i: (bi, i, 0)), aug_spec, aug_spec],
        out_shape=[jax.ShapeDtypeStruct((b, s_len, d), BF16), aug_shape, aug_shape],
        scratch_shapes=[pltpu.VMEM((1, LANES), F32)],
        compiler_params=_params("arbitrary", "arbitrary"),
        name="norm_gate",
    )(x3, g, w_in, b_fg)


def _inproj_kernel(h_ref, wa_ref, wb_ref, *rest, npos, tile_of, kinds):
    z_ref, w_scr = rest[-2:]
    tables = rest[:-2]
    j = pl.program_id(0)
    i = pl.program_id(1)
    tm = h_ref.shape[0]
    shift = HEADS_PER_GROUP
    tile = tile_of(j)

    def slabs():
        for s in range(HEADS_PER_GROUP):
            yield s, slice(s * HEAD_DIM, (s + 1) * HEAD_DIM)

    @pl.when((i == 0) & (tile < GATE_TILE))
    def _():
        w_scr[...] = wa_ref[...].astype(BF16)

    @pl.when((i == 0) & (tile >= GATE_TILE))
    def _():
        lane = lax.broadcasted_iota(jnp.int32, (wa_ref.shape[0], LANES), 1)
        for s, sl in slabs():
            cur = wa_ref[:, sl]
            nxt = wa_ref[:, (s + 1) * HEAD_DIM:(s + 2) * HEAD_DIM] if s + 1 < HEADS_PER_GROUP else wb_ref[...]
            moved = jnp.where(lane < LANES - shift, pltpu.roll(cur, LANES - shift, 1),
                              pltpu.roll(nxt, LANES - shift, 1))
            w_scr[:, sl] = moved.astype(BF16)

    scale_full = HEAD_DIM ** -0.5 * LOG2E
    scale_half = DIFF_QK_DIM ** -0.5 * LOG2E

    def epilogue(kind, a, pos):
        if kind.startswith("rope_full"):
            cf_ref, sf_ref = tables[0], tables[1]
            a = a * cf_ref[pos, :] + pltpu.roll(a, HEAD_DIM // 2, 1) * sf_ref[pos, :]
        elif kind.startswith("rope_half"):
            ch_ref, sha_ref, shb_ref = tables[2], tables[3], tables[4]
            a = (a * ch_ref[pos, :] + pltpu.roll(a, HEAD_DIM - DIFF_QK_DIM // 2, 1) * sha_ref[pos, :]
                 + pltpu.roll(a, DIFF_QK_DIM // 2, 1) * shb_ref[pos, :])
        if kind.endswith("scale"):
            a = a * (scale_half if "half" in kind else scale_full)
        return a

    def run(kind):
        for r in range(tm // MM_CHUNK):
            rows = slice(r * MM_CHUNK, (r + 1) * MM_CHUNK)
            acc = _dot(h_ref[rows, :], w_scr[...])
            pos = pl.ds(pl.multiple_of((i % npos) * tm + r * MM_CHUNK, MM_CHUNK), MM_CHUNK)
            for s, sl in slabs():
                if kind == "transposed":
                    z_ref[0, s, :, rows] = jnp.transpose(acc[:, sl]).astype(BF16)
                else:
                    z_ref[0, s, rows, :] = epilogue(kind, acc[:, sl], pos).astype(BF16)

    for kind in sorted(set(kinds)):
        cond = None
        for jj, kk in enumerate(kinds):
            if kk == kind:
                cond = (j == jj) if cond is None else (cond | (j == jj))
        pl.when(cond)(functools.partial(run, kind))


def _inproj(h2, w_in, layer, tables, b, s_len, tile_of, kinds):
    m, d = h2.shape
    tm = min(MM_ROWS, s_len)
    assert tm % MM_CHUNK == 0
    npos = s_len // tm
    transposed = kinds[0] == "transposed"
    n_heads = len(kinds) * HEADS_PER_GROUP
    if transposed:
        out_spec = pl.BlockSpec((1, HEADS_PER_GROUP, HEAD_DIM, tm), lambda j, i: (i // npos, j, 0, i % npos))
        out_shape = jax.ShapeDtypeStruct((b, n_heads, HEAD_DIM, s_len), BF16)
    else:
        out_spec = pl.BlockSpec((1, HEADS_PER_GROUP, tm, HEAD_DIM), lambda j, i: (i // npos, j, i % npos, 0))
        out_shape = jax.ShapeDtypeStruct((b, n_heads, s_len, HEAD_DIM), BF16)
    tab_spec = pl.BlockSpec((s_len, LANES), lambda j, i: (0, 0))
    return pl.pallas_call(
        functools.partial(_inproj_kernel, npos=npos, tile_of=tile_of, kinds=kinds),
        grid=(len(kinds), m // tm),
        in_specs=[
            pl.BlockSpec((tm, d), lambda j, i: (i, 0)),
            pl.BlockSpec((None, d, GROUP_WIDTH), lambda j, i: (layer, 0, tile_of(j))),
            pl.BlockSpec((None, d, LANES), lambda j, i: (layer, 0, (tile_of(j) + 1) * HEADS_PER_GROUP)),
            *([tab_spec] * len(tables)),
        ],
        out_specs=out_spec,
        out_shape=out_shape,
        scratch_shapes=[pltpu.VMEM((d, GROUP_WIDTH), BF16)],
        compiler_params=_params("arbitrary", "arbitrary"),
        name="inproj_v" if transposed else "inproj_qk",
    )(h2, w_in, w_in, *tables)


def _key_query_mask(t, strict=False):
    key = lax.broadcasted_iota(jnp.int32, (t, t), 0)
    query = lax.broadcasted_iota(jnp.int32, (t, t), 1)
    return (key < query) if strict else (key <= query)


def _softmax_weights(logits):
    m = None
    for s in logits:
        mj = jnp.max(s, axis=0, keepdims=True)
        m = mj if m is None else jnp.maximum(m, mj)
    ps = [jnp.exp2(s - m) for s in logits]
    l = None
    for p in ps:
        lj = jnp.sum(p, axis=0, keepdims=True)
        l = lj if l is None else l + lj
    return ps, l


def _weighted_values(ps, vts):
    acc = None
    for p, vt in zip(ps, vts):
        aj = _dot(vt, p.astype(BF16))
        acc = aj if acc is None else acc + aj
    return acc


def _pipelined_tiles(n_tiles, head, tail):
    ahead = head(0)
    for i in range(n_tiles):
        state = ahead
        if i + 1 < n_tiles:
            ahead = head(i + 1)
        tail(i, state)


def _fox_kernel(q_ref, k_ref, vt_ref, cq_ref, ck_ref, o_ref, *, t):
    s_len = q_ref.shape[2]
    diag = _key_query_mask(t)

    def keys(a, b):
        return jnp.concatenate([k_ref[0, 0, a:b, :], ck_ref[0, 0, a:b, :]], axis=1)

    def head(i):
        lo, hi = i * t, (i + 1) * t
        q = jnp.concatenate([q_ref[0, 0, lo:hi, :], cq_ref[0, 0, lo:hi, :]], axis=1)
        logits, vts = [], []
        if i > 0:
            logits.append(_nt_dot(keys(0, lo), q))
            vts.append(vt_ref[0, 0, :, :lo])
        logits.append(jnp.where(diag, _nt_dot(keys(lo, hi), q), NEG_INF))
        vts.append(vt_ref[0, 0, :, lo:hi])
        return logits, vts

    def tail(i, state):
        logits, vts = state
        ps, l = _softmax_weights(logits)
        out_t = _weighted_values(ps, vts) * (1.0 / l)
        o_ref[0, 0, i * t:(i + 1) * t, :] = jnp.transpose(out_t).astype(o_ref.dtype)

    _pipelined_tiles(s_len // t, head, tail)


def _moba_kernel(q_ref, k_ref, vt_ref, o_ref, *, t):
    s_len = q_ref.shape[2]

    brow = lax.broadcasted_iota(jnp.int32, (BF16_ROWS, s_len), 0)
    bcol = lax.broadcasted_iota(jnp.int32, (BF16_ROWS, s_len), 1)
    member = jnp.where(bcol // t == brow, 1.0, 0.0).astype(BF16)
    k_mean = _dot(member, k_ref[0, 0]) * (1.0 / t)
    km_parts = _split3_bf16(k_mean)
    block = lax.broadcasted_iota(jnp.int32, (BF16_ROWS, t), 0)
    diag = _key_query_mask(t)

    def head(i):
        lo, hi = i * t, (i + 1) * t
        q = q_ref[0, 0, lo:hi, :]
        s_past = _nt_dot(k_ref[0, 0, :lo, :], q) if i > 0 else None
        s_diag = _nt_dot(k_ref[0, 0, lo:hi, :], q)
        gate = sum(_nt_dot(part, q) for part in km_parts) if i > MOBA_TOPK else None
        return s_past, s_diag, gate

    def tail(i, state):
        lo, hi = i * t, (i + 1) * t
        s_past, s_diag, gate = state
        logits, vts = [], []
        if 0 < i <= MOBA_TOPK:
            logits.append(s_past)
            vts.append(vt_ref[0, 0, :, :lo])
        elif i > MOBA_TOPK:
            gate = jnp.where(block < i, gate, NEG_INF)
            beaten = jnp.zeros((BF16_ROWS, t), F32)
            for mblk in range(i):
                gm = gate[mblk:mblk + 1, :]
                wins = jnp.where(gm > gate, 1.0, jnp.where((gm == gate) & (mblk < block), 1.0, 0.0))
                beaten = beaten + wins
            for n in range(i):
                keep_n = beaten[n:n + 1, :] < MOBA_TOPK
                logits.append(jnp.where(keep_n, s_past[n * t:(n + 1) * t, :], NEG_INF))
                vts.append(vt_ref[0, 0, :, n * t:(n + 1) * t])
        logits.append(jnp.where(diag, s_diag, NEG_INF))
        vts.append(vt_ref[0, 0, :, lo:hi])
        ps, l = _softmax_weights(logits)
        out_t = _weighted_values(ps, vts) * (1.0 / l)
        o_ref[0, 0, lo:hi, :] = jnp.transpose(out_t).astype(o_ref.dtype)

    _pipelined_tiles(s_len // t, head, tail)


def _diff_kernel(lam_init_ref, q_ref, k_ref, vt_ref, lq1_ref, lk1_ref, lq2_ref, lk2_ref, gsub_ref, o_ref,
                 *, t):
    s_len = q_ref.shape[2]
    lam_init = lam_init_ref[0]
    lam = (jnp.exp(jnp.sum(lq1_ref[...] * lk1_ref[...], axis=-1, keepdims=True))
           - jnp.exp(jnp.sum(lq2_ref[...] * lk2_ref[...], axis=-1, keepdims=True)) + lam_init)
    lane = lax.broadcasted_iota(jnp.int32, (t, LANES), 1)
    diag = _key_query_mask(t)

    def head(i):
        lo, hi = i * t, (i + 1) * t
        q = q_ref[0, 0, lo:hi, :]
        zero = jnp.zeros_like(q)
        halves = (jnp.where(lane < DIFF_QK_DIM, q, zero), jnp.where(lane >= DIFF_QK_DIM, q, zero))
        both = []
        for qh in halves:
            logits = []
            if i > 0:
                logits.append(_nt_dot(k_ref[0, 0, :lo, :], qh))
            logits.append(jnp.where(diag, _nt_dot(k_ref[0, 0, lo:hi, :], qh), NEG_INF))
            both.append(logits)
        return both

    def tail(i, both):
        lo, hi = i * t, (i + 1) * t
        (ps1, l1), (ps2, l2) = (_softmax_weights(logits) for logits in both)
        r1 = 1.0 / l1
        r2 = lam / l2
        ps = [p1 * r1 - p2 * r2 for p1, p2 in zip(ps1, ps2)]
        vts = ([vt_ref[0, 0, :, :lo]] if i > 0 else []) + [vt_ref[0, 0, :, lo:hi]]
        out = jnp.transpose(_weighted_values(ps, vts))
        y = _rms_normalize(out, gsub_ref[...], DIFF_SUBLN_EPS)
        o_ref[0, 0, lo:hi, :] = (y * (1.0 - lam_init)).astype(o_ref.dtype)

    _pipelined_tiles(s_len // t, head, tail)


def _sb_kernel(q_ref, k_ref, vt_ref, o_ref, *, t):
    s_len = q_ref.shape[2]
    row = lax.broadcasted_iota(jnp.int32, (t, t), 0)
    col = lax.broadcasted_iota(jnp.int32, (t, t), 1)
    later = jnp.where(col > row, 1.0, 0.0).astype(BF16)
    strict = _key_query_mask(t, strict=True)

    def head(i):
        hi = (i + 1) * t
        z_all = _nt_dot(k_ref[0, 0, :hi, :], q_ref[0, 0, i * t:hi, :])
        blocks = []
        for n in range(i, -1, -1):
            z = z_all[n * t:(n + 1) * t, :]
            sp = jnp.log2(1.0 + jnp.exp2(-jnp.abs(z)))
            log_beta = jnp.minimum(z, 0.0) - sp
            log_1m = log_beta - z
            if n == i:
                log_1m = jnp.where(strict, log_1m, 0.0)
            l_hi, l_lo = _split2_bf16(log_1m)
            within = _dot(later, l_hi) + _dot(later, l_lo)
            blocks.append((n, log_beta, within, jnp.sum(log_1m, axis=0, keepdims=True)))
        return blocks

    def tail(i, blocks):
        run = None
        acc = None
        for n, log_beta, within, total in blocks:
            between = within if run is None else within + run
            a = jnp.exp2(log_beta + between)
            if n == i:
                a = jnp.where(strict, a, 0.0)
            an = _dot(vt_ref[0, 0, :, n * t:(n + 1) * t], a.astype(BF16))
            acc = an if acc is None else acc + an
            run = total if run is None else run + total
        o_ref[0, 0, i * t:(i + 1) * t, :] = jnp.transpose(acc).astype(o_ref.dtype)

    _pipelined_tiles(s_len // t, head, tail)


def _attention_call(kernel_fn, name, zqk, zvt, mixer, extra_inputs=(), extra_specs=(), prefix_inputs=(),
                    prefix_specs=()):
    b, _, s_len, _ = zqk.shape
    t = min(ATTN_TILE, s_len)
    hq = 2 * mixer * HEADS_PER_GROUP
    hk = hq + HEADS_PER_GROUP
    hv = mixer * HEADS_PER_GROUP
    seq_spec = lambda h0: pl.BlockSpec((1, 1, s_len, HEAD_DIM), lambda bi, h: (bi, h0 + h, 0, 0))
    return pl.pallas_call(
        functools.partial(kernel_fn, t=t),
        grid=(b, HEADS_PER_GROUP),
        in_specs=[*prefix_specs, seq_spec(hq), seq_spec(hk),
                  pl.BlockSpec((1, 1, HEAD_DIM, s_len), lambda bi, h: (bi, hv + h, 0, 0)), *extra_specs],
        out_specs=seq_spec(0),
        out_shape=jax.ShapeDtypeStruct((b, HEADS_PER_GROUP, s_len, HEAD_DIM), BF16),
        compiler_params=_params("arbitrary", "arbitrary"),
        name=name,
    )(*prefix_inputs, zqk, zqk, zvt, *extra_inputs)


def _fox_attention(zqk, zvt, cq, ck):
    s_len = zqk.shape[2]
    aug_spec = pl.BlockSpec((1, 1, s_len, LANES), lambda bi, h: (bi, h, 0, 0))
    return _attention_call(_fox_kernel, "fox", zqk, zvt, MIXER_FOX, (cq, ck), (aug_spec, aug_spec))


def _moba_attention(zqk, zvt):
    s_len = zqk.shape[2]
    assert ATTN_TILE == MOBA_BLOCK and s_len % MOBA_BLOCK == 0 and s_len // MOBA_BLOCK <= BF16_ROWS
    return _attention_call(_moba_kernel, "moba", zqk, zvt, MIXER_MOBA)


def _diff_attention(zqk, zvt, lq1, lk1, lq2, lk2, gsub, lam_init):
    vec = lambda n: pl.BlockSpec((1, n), lambda bi, h: (0, 0))
    return _attention_call(
        _diff_kernel, "diff", zqk, zvt, MIXER_DIFF, (lq1, lk1, lq2, lk2, gsub),
        (vec(DIFF_QK_DIM), vec(DIFF_QK_DIM), vec(DIFF_QK_DIM), vec(DIFF_QK_DIM), vec(HEAD_DIM)),
        prefix_inputs=(jnp.full((1,), lam_init, F32),),
        prefix_specs=(pl.BlockSpec(memory_space=pltpu.SMEM),))


def _sb_attention(zqk, zvt):
    return _attention_call(_sb_kernel, "stickbreak", zqk, zvt, MIXER_SB)


def _cast_weights_once(w_ref, w_scr):
    @pl.when(pl.program_id(1) == 0)
    def _():
        w_scr[...] = w_ref[...].astype(BF16)


def _outproj_kernel(a0_ref, a1_ref, a2_ref, a3_ref, w_ref, x_ref, o_ref, w_scr):
    _cast_weights_once(w_ref, w_scr)
    heads = [a_ref[0, s] for a_ref in (a0_ref, a1_ref, a2_ref, a3_ref) for s in range(HEADS_PER_GROUP)]
    mixed = jnp.concatenate(heads, axis=-1)
    o_ref[...] = x_ref[...] + _dot(mixed, w_scr[...])


def _outproj(mixed, w, layer, x2, s_len, tn=512):
    m, d = x2.shape
    tm = min(MM_ROWS, s_len)
    npos = s_len // tm
    a_spec = pl.BlockSpec((1, HEADS_PER_GROUP, tm, HEAD_DIM), lambda j, i: (i // npos, 0, i % npos, 0))
    return pl.pallas_call(
        _outproj_kernel,
        grid=(d // tn, m // tm),
        in_specs=[a_spec, a_spec, a_spec, a_spec,
                  pl.BlockSpec((None, w.shape[1], tn), lambda j, i: (layer, 0, j)),
                  pl.BlockSpec((tm, tn), lambda j, i: (i, j))],
        out_specs=pl.BlockSpec((tm, tn), lambda j, i: (i, j)),
        out_shape=jax.ShapeDtypeStruct((m, d), F32),
        scratch_shapes=[pltpu.VMEM((w.shape[1], tn), BF16)],
        compiler_params=_params("arbitrary", "arbitrary"),
        name="outproj",
    )(*mixed, w, x2)


def _ffn_up_kernel(h_ref, wg_ref, wu_ref, o_ref, wg_scr, wu_scr):
    _cast_weights_once(wg_ref, wg_scr)
    _cast_weights_once(wu_ref, wu_scr)
    h = h_ref[...]
    gate = _dot(h, wg_scr[...])
    up = _dot(h, wu_scr[...])
    o_ref[...] = (gate / (1.0 + jnp.exp(-gate)) * up).astype(BF16)


def _ffn_up(h2, wg, wu, layer, tn=512):
    m, d = h2.shape
    f = wg.shape[2]
    tm = min(MM_ROWS, m)
    w_spec = pl.BlockSpec((None, d, tn), lambda j, i: (layer, 0, j))
    return pl.pallas_call(
        _ffn_up_kernel,
        grid=(f // tn, m // tm),
        in_specs=[pl.BlockSpec((tm, d), lambda j, i: (i, 0)), w_spec, w_spec],
        out_specs=pl.BlockSpec((tm, tn), lambda j, i: (i, j)),
        out_shape=jax.ShapeDtypeStruct((m, f), BF16),
        scratch_shapes=[pltpu.VMEM((d, tn), BF16), pltpu.VMEM((d, tn), BF16)],
        compiler_params=_params("arbitrary", "arbitrary"),
        name="ffn_up",
    )(h2, wg, wu)


def _ffn_down_kernel(a_ref, w_ref, x_ref, o_ref, w_scr):
    _cast_weights_once(w_ref, w_scr)
    o_ref[...] = x_ref[...] + _dot(a_ref[...], w_scr[...])


def _ffn_down(act, w, layer, x2, tn=512):
    m, d = x2.shape
    f = act.shape[1]
    tm = min(DOWN_ROWS, m)
    return pl.pallas_call(
        _ffn_down_kernel,
        grid=(d // tn, m // tm),
        in_specs=[pl.BlockSpec((tm, f), lambda j, i: (i, 0)),
                  pl.BlockSpec((None, f, tn), lambda j, i: (layer, 0, j)),
                  pl.BlockSpec((tm, tn), lambda j, i: (i, j))],
        out_specs=pl.BlockSpec((tm, tn), lambda j, i: (i, j)),
        out_shape=jax.ShapeDtypeStruct((m, d), F32),
        scratch_shapes=[pltpu.VMEM((f, tn), BF16)],
        compiler_params=_params("arbitrary", "arbitrary"),
        name="ffn_down",
    )(act, w, x2)


def _rope_tables(s_len):
    def cos_sin(dim):
        inv = 1.0 / (ROPE_THETA ** (jnp.arange(0, dim, 2, dtype=F32) / dim))
        ang = jnp.arange(s_len, dtype=F32)[:, None] * inv[None, :]
        return jnp.cos(ang), jnp.sin(ang)

    cf, sf = cos_sin(HEAD_DIM)
    ch, sh = cos_sin(DIFF_QK_DIM)
    zh = jnp.zeros_like(sh)
    return (jnp.concatenate([cf, cf], axis=-1),
            jnp.concatenate([-sf, sf], axis=-1),
            jnp.concatenate([ch, ch, ch, ch], axis=-1),
            jnp.concatenate([-sh, zh, -sh, zh], axis=-1),
            jnp.concatenate([zh, sh, zh, sh], axis=-1))


def kernel(x, w_in, b_fgate, w_out, diff_lq1, diff_lk1, diff_lq2, diff_lk2, diff_subln, attn_norm,
           w_gate, w_up, w_down, ffn_norm, final_norm):
    b, s_len, d = x.shape
    depth = w_in.shape[0]
    m = b * s_len
    tables = _rope_tables(s_len)
    b_fg = jnp.pad(b_fgate, ((0, 0), (0, LANES - HEADS_PER_GROUP)))
    qk_tile = lambda j: j + j // 2
    v_tile = lambda j: 3 * j + 2

    x2 = x.reshape(m, d)
    for l in range(depth):
        h, cq, ck = _norm_gate(x2.reshape(b, s_len, d), attn_norm[l][None, :], w_in, l, b_fg[l][None, :])
        h2 = h.reshape(m, d)
        zqk = _inproj(h2, w_in, l, tables, b, s_len, qk_tile, QK_KINDS)
        zvt = _inproj(h2, w_in, l, (), b, s_len, v_tile, ("transposed",) * N_MIXERS)
        lam_init = 0.8 - 0.6 * math.exp(-0.3 * l)
        fox = _fox_attention(zqk, zvt, cq, ck)
        moba = _moba_attention(zqk, zvt)
        diff = _diff_attention(zqk, zvt, diff_lq1[l][None, :], diff_lk1[l][None, :], diff_lq2[l][None, :],
                               diff_lk2[l][None, :], diff_subln[l][None, :], lam_init)
        sb = _sb_attention(zqk, zvt)
        x2 = _outproj((fox, moba, diff, sb), w_out, l, x2, s_len)
        hf = _norm(x2, ffn_norm[l][None, :], BF16)
        act = _ffn_up(hf, w_gate, w_up, l)
        x2 = _ffn_down(act, w_down, l, x2)
    return _norm(x2, final_norm[None, :], F32).reshape(b, s_len, d)
```

```python
import functools
import math

import jax
import jax.numpy as jnp
from jax import lax
from jax.experimental import pallas as pl
from jax.experimental.pallas import tpu as pltpu

F32 = jnp.float32
BF16 = jnp.bfloat16

HEAD_DIM = 128
HEADS_PER_GROUP = 4
GROUP_WIDTH = HEADS_PER_GROUP * HEAD_DIM
N_MIXERS = 4
GATE_TILE = 3
DIFF_QK_DIM = HEAD_DIM // 2
MOBA_BLOCK = 256
MOBA_TOPK = 3
ROPE_THETA = 10000.0
NORM_EPS = 1e-6
DIFF_SUBLN_EPS = 1e-5
LANES = 128
SUBLANES = 8
BF16_ROWS = 16
NEG_INF = float("-inf")
LOG2E = math.log2(math.e)
ATTN_TILE = 256
NORM_ROWS = 512
MM_ROWS = 2048
MM_CHUNK = 256
DOWN_ROWS = 512

VMEM_LIMIT_BYTES = 56 * 1024 * 1024

MIXER_FOX, MIXER_MOBA, MIXER_DIFF, MIXER_SB = range(N_MIXERS)
QK_KINDS = ("scale", "plain", "rope_full_scale", "rope_full", "rope_half_scale", "rope_half", "scale", "plain")


def _nt_dot(a, b):
    return lax.dot_general(a, b, (((1,), (1,)), ((), ())), preferred_element_type=F32)


def _dot(a, b):
    return jnp.dot(a, b, preferred_element_type=F32)


def _split2_bf16(x):
    hi = x.astype(BF16)
    lo = (x - hi.astype(F32)).astype(BF16)
    return hi, lo


def _split3_bf16(x):
    hi = x.astype(BF16)
    r = x - hi.astype(F32)
    mid = r.astype(BF16)
    lo = (r - mid.astype(F32)).astype(BF16)
    return hi, mid, lo


def _rms_normalize(x, g, eps):
    ms = jnp.mean(x * x, axis=-1, keepdims=True)
    return x * lax.rsqrt(ms + eps) * g


def _params(*semantics):
    return pltpu.CompilerParams(dimension_semantics=semantics, vmem_limit_bytes=VMEM_LIMIT_BYTES)


def _norm_kernel(x_ref, g_ref, o_ref):
    o_ref[...] = _rms_normalize(x_ref[...], g_ref[...], NORM_EPS).astype(o_ref.dtype)


def _norm(x2, g, out_dtype):
    m, d = x2.shape
    tm = min(NORM_ROWS, m)
    return pl.pallas_call(
        _norm_kernel,
        grid=(m // tm,),
        in_specs=[pl.BlockSpec((tm, d), lambda i: (i, 0)), pl.BlockSpec((1, d), lambda i: (0, 0))],
        out_specs=pl.BlockSpec((tm, d), lambda i: (i, 0)),
        out_shape=jax.ShapeDtypeStruct((m, d), out_dtype),
        compiler_params=_params("arbitrary"),
        name="norm",
    )(x2, g)


def _norm_gate_kernel(x_ref, g_ref, wfg_ref, b_ref, h_ref, cq_ref, ck_ref, carry_ref):
    tm = x_ref.shape[1]

    @pl.when(pl.program_id(1) == 0)
    def _():
        carry_ref[...] = jnp.zeros_like(carry_ref)

    h = _rms_normalize(x_ref[0], g_ref[...], NORM_EPS).astype(BF16)
    h_ref[0] = h
    f = _dot(h, wfg_ref[...].astype(BF16)) + b_ref[...]
    ls = jnp.minimum(f, 0.0) - jnp.log1p(jnp.exp(-jnp.abs(f)))
    row = lax.broadcasted_iota(jnp.int32, (tm, tm), 0)
    col = lax.broadcasted_iota(jnp.int32, (tm, tm), 1)
    tri = jnp.where(row >= col, 1.0, 0.0).astype(BF16)
    hi, mid, lo = _split3_bf16(ls)
    c = _dot(tri, hi) + _dot(tri, mid) + _dot(tri, lo) + carry_ref[...]
    carry_ref[...] = c[tm - 1:tm, :]

    lane = lax.broadcasted_iota(jnp.int32, (tm, LANES), 1)
    for head in range(HEADS_PER_GROUP):
        parts = _split3_bf16(jnp.broadcast_to(c[:, head:head + 1] * LOG2E, (tm, LANES)))
        n_parts = len(parts)
        q_cols = jnp.where(lane < 2 * n_parts, 1.0, 0.0)
        k_cols = jnp.where(lane < n_parts, 1.0, 0.0)
        for n, part in enumerate(parts):
            q_cols = jnp.where(lane == n, part.astype(F32), q_cols)
            k_cols = jnp.where(lane == n_parts + n, -part.astype(F32), k_cols)
        cq_ref[0, head] = q_cols.astype(BF16)
        ck_ref[0, head] = k_cols.astype(BF16)


def _norm_gate(x3, g, w_in, layer, b_fg):
    b, s_len, d = x3.shape
    tm = min(NORM_ROWS, s_len)
    gate_block = GATE_TILE * GROUP_WIDTH // LANES
    aug_spec = pl.BlockSpec((1, HEADS_PER_GROUP, tm, LANES), lambda bi, i: (bi, 0, i, 0))
    aug_shape = jax.ShapeDtypeStruct((b, HEADS_PER_GROUP, s_len, LANES), BF16)
    return pl.pallas_call(
        _norm_gate_kernel,
        grid=(b, s_len // tm),
        in_specs=[pl.BlockSpec((1, tm, d), lambda bi, i: (bi, i, 0)),
                  pl.BlockSpec((1, d), lambda bi, i: (0, 0)),
                  pl.BlockSpec((None, d, LANES), lambda bi, i: (layer, 0, gate_block)),
                  pl.BlockSpec((1, LANES), lambda bi, i: (0, 0))],
        out_specs=[pl.BlockSpec((1, tm, d), lambda bi, i: (bi, i, 0)), aug_spec, aug_spec],
        out_shape=[jax.ShapeDtypeStruct((b, s_len, d), BF16), aug_shape, aug_shape],
        scratch_shapes=[pltpu.VMEM((1, LANES), F32)],
        compiler_params=_params("arbitrary", "arbitrary"),
        name="norm_gate",
    )(x3, g, w_in, b_fg)


def _inproj_kernel(h_ref, wa_ref, wb_ref, *rest, npos, tile_of, kinds):
    z_ref, w_scr = rest[-2:]
    tables = rest[:-2]
    j = pl.program_id(0)
    i = pl.program_id(1)
    tm = h_ref.shape[0]
    shift = HEADS_PER_GROUP
    tile = tile_of(j)

    def slabs():
        for s in range(HEADS_PER_GROUP):
            yield s, slice(s * HEAD_DIM, (s + 1) * HEAD_DIM)

    @pl.when((i == 0) & (tile < GATE_TILE))
    def _():
        w_scr[...] = wa_ref[...].astype(BF16)

    @pl.when((i == 0) & (tile >= GATE_TILE))
    def _():
        lane = lax.broadcasted_iota(jnp.int32, (wa_ref.shape[0], LANES), 1)
        for s, sl in slabs():
            cur = wa_ref[:, sl]
            nxt = wa_ref[:, (s + 1) * HEAD_DIM:(s + 2) * HEAD_DIM] if s + 1 < HEADS_PER_GROUP else wb_ref[...]
            moved = jnp.where(lane < LANES - shift, pltpu.roll(cur, LANES - shift, 1),
                              pltpu.roll(nxt, LANES - shift, 1))
            w_scr[:, sl] = moved.astype(BF16)

    scale_full = HEAD_DIM ** -0.5 * LOG2E
    scale_half = DIFF_QK_DIM ** -0.5 * LOG2E

    def epilogue(kind, a, pos):
        if kind.startswith("rope_full"):
            cf_ref, sf_ref = tables[0], tables[1]
            a = a * cf_ref[pos, :] + pltpu.roll(a, HEAD_DIM // 2, 1) * sf_ref[pos, :]
        elif kind.startswith("rope_half"):
            ch_ref, sha_ref, shb_ref = tables[2], tables[3], tables[4]
            a = (a * ch_ref[pos, :] + pltpu.roll(a, HEAD_DIM - DIFF_QK_DIM // 2, 1) * sha_ref[pos, :]
                 + pltpu.roll(a, DIFF_QK_DIM // 2, 1) * shb_ref[pos, :])
        if kind.endswith("scale"):
            a = a * (scale_half if "half" in kind else scale_full)
        return a

    def run(kind):
        for r in range(tm // MM_CHUNK):
            rows = slice(r * MM_CHUNK, (r + 1) * MM_CHUNK)
            acc = _dot(h_ref[rows, :], w_scr[...])
            pos = pl.ds(pl.multiple_of((i % npos) * tm + r * MM_CHUNK, MM_CHUNK), MM_CHUNK)
            for s, sl in slabs():
                if kind == "transposed":
                    z_ref[0, s, :, rows] = jnp.transpose(acc[:, sl]).astype(BF16)
                else:
                    z_ref[0, s, rows, :] = epilogue(kind, acc[:, sl], pos).astype(BF16)

    for kind in sorted(set(kinds)):
        cond = None
        for jj, kk in enumerate(kinds):
            if kk == kind:
                cond = (j == jj) if cond is None else (cond | (j == jj))
        pl.when(cond)(functools.partial(run, kind))


def _inproj(h2, w_in, layer, tables, b, s_len, tile_of, kinds):
    m, d = h2.shape
    tm = min(MM_ROWS, s_len)
    assert tm % MM_CHUNK == 0
    npos = s_len // tm
    transposed = kinds[0] == "transposed"
    n_heads = len(kinds) * HEADS_PER_GROUP
    if transposed:
        out_spec = pl.BlockSpec((1, HEADS_PER_GROUP, HEAD_DIM, tm), lambda j, i: (i // npos, j, 0, i % npos))
        out_shape = jax.ShapeDtypeStruct((b, n_heads, HEAD_DIM, s_len), BF16)
    else:
        out_spec = pl.BlockSpec((1, HEADS_PER_GROUP, tm, HEAD_DIM), lambda j, i: (i // npos, j, i % npos, 0))
        out_shape = jax.ShapeDtypeStruct((b, n_heads, s_len, HEAD_DIM), BF16)
    tab_spec = pl.BlockSpec((s_len, LANES), lambda j, i: (0, 0))
    return pl.pallas_call(
        functools.partial(_inproj_kernel, npos=npos, tile_of=tile_of, kinds=kinds),
        grid=(len(kinds), m // tm),
        in_specs=[
            pl.BlockSpec((tm, d), lambda j, i: (i, 0)),
            pl.BlockSpec((None, d, GROUP_WIDTH), lambda j, i: (layer, 0, tile_of(j))),
            pl.BlockSpec((None, d, LANES), lambda j, i: (layer, 0, (tile_of(j) + 1) * HEADS_PER_GROUP)),
            *([tab_spec] * len(tables)),
        ],
        out_specs=out_spec,
        out_shape=out_shape,
        scratch_shapes=[pltpu.VMEM((d, GROUP_WIDTH), BF16)],
        compiler_params=_params("arbitrary", "arbitrary"),
        name="inproj_v" if transposed else "inproj_qk",
    )(h2, w_in, w_in, *tables)


def _key_query_mask(t, strict=False):
    key = lax.broadcasted_iota(jnp.int32, (t, t), 0)
    query = lax.broadcasted_iota(jnp.int32, (t, t), 1)
    return (key < query) if strict else (key <= query)


def _softmax_weights(logits):
    m = None
    for s in logits:
        mj = jnp.max(s, axis=0, keepdims=True)
        m = mj if m is None else jnp.maximum(m, mj)
    ps = [jnp.exp2(s - m) for s in logits]
    l = None
    for p in ps:
        lj = jnp.sum(p, axis=0, keepdims=True)
        l = lj if l is None else l + lj
    return ps, l


def _weighted_values(ps, vts):
    acc = None
    for p, vt in zip(ps, vts):
        aj = _dot(vt, p.astype(BF16))
        acc = aj if acc is None else acc + aj
    return acc


def _pipelined_tiles(n_tiles, head, tail, depth=1):
    states = {i: head(i) for i in range(min(depth, n_tiles))}
    for i in range(n_tiles):
        if i + depth < n_tiles:
            states[i + depth] = head(i + depth)
        tail(i, states.pop(i))


def _fox_kernel(q_ref, k_ref, vt_ref, cq_ref, ck_ref, o_ref, *, t):
    s_len = q_ref.shape[2]
    diag = _key_query_mask(t)

    def keys(a, b):
        return jnp.concatenate([k_ref[0, 0, a:b, :], ck_ref[0, 0, a:b, :]], axis=1)

    def head(i):
        lo, hi = i * t, (i + 1) * t
        q = jnp.concatenate([q_ref[0, 0, lo:hi, :], cq_ref[0, 0, lo:hi, :]], axis=1)
        logits, vts = [], []
        if i > 0:
            logits.append(_nt_dot(keys(0, lo), q))
            vts.append(vt_ref[0, 0, :, :lo])
        logits.append(jnp.where(diag, _nt_dot(keys(lo, hi), q), NEG_INF))
        vts.append(vt_ref[0, 0, :, lo:hi])
        return logits, vts

    def tail(i, state):
        logits, vts = state
        ps, l = _softmax_weights(logits)
        out_t = _weighted_values(ps, vts) * (1.0 / l)
        o_ref[0, 0, i * t:(i + 1) * t, :] = jnp.transpose(out_t).astype(o_ref.dtype)

    _pipelined_tiles(s_len // t, head, tail, depth=2)


def _moba_kernel(q_ref, k_ref, vt_ref, o_ref, *, t):
    s_len = q_ref.shape[2]

    brow = lax.broadcasted_iota(jnp.int32, (BF16_ROWS, s_len), 0)
    bcol = lax.broadcasted_iota(jnp.int32, (BF16_ROWS, s_len), 1)
    member = jnp.where(bcol // t == brow, 1.0, 0.0).astype(BF16)
    k_mean = _dot(member, k_ref[0, 0]) * (1.0 / t)
    km_parts = _split3_bf16(k_mean)
    block = lax.broadcasted_iota(jnp.int32, (BF16_ROWS, t), 0)
    diag = _key_query_mask(t)

    def head(i):
        lo, hi = i * t, (i + 1) * t
        q = q_ref[0, 0, lo:hi, :]
        s_past = _nt_dot(k_ref[0, 0, :lo, :], q) if i > 0 else None
        s_diag = _nt_dot(k_ref[0, 0, lo:hi, :], q)
        gate = sum(_nt_dot(part, q) for part in km_parts) if i > MOBA_TOPK else None
        return s_past, s_diag, gate

    def tail(i, state):
        lo, hi = i * t, (i + 1) * t
        s_past, s_diag, gate = state
        logits, vts = [], []
        if 0 < i <= MOBA_TOPK:
            logits.append(s_past)
            vts.append(vt_ref[0, 0, :, :lo])
        elif i > MOBA_TOPK:
            gate = jnp.where(block < i, gate, NEG_INF)
            beaten = jnp.zeros((BF16_ROWS, t), F32)
            for mblk in range(i):
                gm = gate[mblk:mblk + 1, :]
                wins = jnp.where(gm > gate, 1.0, jnp.where((gm == gate) & (mblk < block), 1.0, 0.0))
                beaten = beaten + wins
            for n in range(i):
                keep_n = beaten[n:n + 1, :] < MOBA_TOPK
                logits.append(jnp.where(keep_n, s_past[n * t:(n + 1) * t, :], NEG_INF))
                vts.append(vt_ref[0, 0, :, n * t:(n + 1) * t])
        logits.append(jnp.where(diag, s_diag, NEG_INF))
        vts.append(vt_ref[0, 0, :, lo:hi])
        ps, l = _softmax_weights(logits)
        out_t = _weighted_values(ps, vts) * (1.0 / l)
        o_ref[0, 0, lo:hi, :] = jnp.transpose(out_t).astype(o_ref.dtype)

    _pipelined_tiles(s_len // t, head, tail, depth=2)


def _diff_kernel(lam_init_ref, q_ref, k_ref, vt_ref, lq1_ref, lk1_ref, lq2_ref, lk2_ref, gsub_ref, o_ref,
                 *, t):
    s_len = q_ref.shape[2]
    lam_init = lam_init_ref[0]
    lam = (jnp.exp(jnp.sum(lq1_ref[...] * lk1_ref[...], axis=-1, keepdims=True))
           - jnp.exp(jnp.sum(lq2_ref[...] * lk2_ref[...], axis=-1, keepdims=True)) + lam_init)
    lane = lax.broadcasted_iota(jnp.int32, (t, LANES), 1)
    diag = _key_query_mask(t)

    def head(i):
        lo, hi = i * t, (i + 1) * t
        q = q_ref[0, 0, lo:hi, :]
        zero = jnp.zeros_like(q)
        halves = (jnp.where(lane < DIFF_QK_DIM, q, zero), jnp.where(lane >= DIFF_QK_DIM, q, zero))
        both = []
        for qh in halves:
            logits = []
            if i > 0:
                logits.append(_nt_dot(k_ref[0, 0, :lo, :], qh))
            logits.append(jnp.where(diag, _nt_dot(k_ref[0, 0, lo:hi, :], qh), NEG_INF))
            both.append(logits)
        return both

    def tail(i, both):
        lo, hi = i * t, (i + 1) * t
        (ps1, l1), (ps2, l2) = (_softmax_weights(logits) for logits in both)
        ratio = lam * l1 / l2
        ps = [p1 - p2 * ratio for p1, p2 in zip(ps1, ps2)]
        vts = ([vt_ref[0, 0, :, :lo]] if i > 0 else []) + [vt_ref[0, 0, :, lo:hi]]
        out = jnp.transpose(_weighted_values(ps, vts) * (1.0 / l1))
        y = _rms_normalize(out, gsub_ref[...], DIFF_SUBLN_EPS)
        o_ref[0, 0, lo:hi, :] = (y * (1.0 - lam_init)).astype(o_ref.dtype)

    _pipelined_tiles(s_len // t, head, tail)


def _sb_kernel(q_ref, k_ref, vt_ref, o_ref, *, t):
    s_len = q_ref.shape[2]
    row = lax.broadcasted_iota(jnp.int32, (t, t), 0)
    col = lax.broadcasted_iota(jnp.int32, (t, t), 1)
    later = jnp.where(col > row, 1.0, 0.0).astype(BF16)
    strict = _key_query_mask(t, strict=True)

    def head(i):
        hi = (i + 1) * t
        z_all = _nt_dot(k_ref[0, 0, :hi, :], q_ref[0, 0, i * t:hi, :])
        blocks = []
        for n in range(i, -1, -1):
            z = z_all[n * t:(n + 1) * t, :]
            sp = jnp.log2(1.0 + jnp.exp2(-jnp.abs(z)))
            log_beta = jnp.minimum(z, 0.0) - sp
            log_1m = log_beta - z
            if n == i:
                log_1m = jnp.where(strict, log_1m, 0.0)
            l_hi, l_lo = _split2_bf16(log_1m)
            within = _dot(later, l_hi) + _dot(later, l_lo)
            blocks.append((n, log_beta, within, within[0:1, :] + log_1m[0:1, :]))
        return blocks

    def tail(i, blocks):
        run = None
        acc = None
        for n, log_beta, within, total in blocks:
            a = jnp.exp2(log_beta + within)
            if n == i:
                a = jnp.where(strict, a, 0.0)
            local = _dot(vt_ref[0, 0, :, n * t:(n + 1) * t], a.astype(BF16))
            acc = local if acc is None else acc + local * jnp.exp2(run)
            run = total if run is None else run + total
        o_ref[0, 0, i * t:(i + 1) * t, :] = jnp.transpose(acc).astype(o_ref.dtype)

    _pipelined_tiles(s_len // t, head, tail)


def _attention_call(kernel_fn, name, zqk, zvt, mixer, extra_inputs=(), extra_specs=(), prefix_inputs=(),
                    prefix_specs=()):
    b, _, s_len, _ = zqk.shape
    t = min(ATTN_TILE, s_len)
    hq = 2 * mixer * HEADS_PER_GROUP
    hk = hq + HEADS_PER_GROUP
    hv = mixer * HEADS_PER_GROUP
    seq_spec = lambda h0: pl.BlockSpec((1, 1, s_len, HEAD_DIM), lambda bi, h: (bi, h0 + h, 0, 0))
    return pl.pallas_call(
        functools.partial(kernel_fn, t=t),
        grid=(b, HEADS_PER_GROUP),
        in_specs=[*prefix_specs, seq_spec(hq), seq_spec(hk),
                  pl.BlockSpec((1, 1, HEAD_DIM, s_len), lambda bi, h: (bi, hv + h, 0, 0)), *extra_specs],
        out_specs=seq_spec(0),
        out_shape=jax.ShapeDtypeStruct((b, HEADS_PER_GROUP, s_len, HEAD_DIM), BF16),
        compiler_params=_params("arbitrary", "arbitrary"),
        name=name,
    )(*prefix_inputs, zqk, zqk, zvt, *extra_inputs)


def _fox_attention(zqk, zvt, cq, ck):
    s_len = zqk.shape[2]
    aug_spec = pl.BlockSpec((1, 1, s_len, LANES), lambda bi, h: (bi, h, 0, 0))
    return _attention_call(_fox_kernel, "fox", zqk, zvt, MIXER_FOX, (cq, ck), (aug_spec, aug_spec))


def _moba_attention(zqk, zvt):
    s_len = zqk.shape[2]
    assert ATTN_TILE == MOBA_BLOCK and s_len % MOBA_BLOCK == 0 and s_len // MOBA_BLOCK <= BF16_ROWS
    return _attention_call(_moba_kernel, "moba", zqk, zvt, MIXER_MOBA)


def _diff_attention(zqk, zvt, lq1, lk1, lq2, lk2, gsub, lam_init):
    vec = lambda n: pl.BlockSpec((1, n), lambda bi, h: (0, 0))
    return _attention_call(
        _diff_kernel, "diff", zqk, zvt, MIXER_DIFF, (lq1, lk1, lq2, lk2, gsub),
        (vec(DIFF_QK_DIM), vec(DIFF_QK_DIM), vec(DIFF_QK_DIM), vec(DIFF_QK_DIM), vec(HEAD_DIM)),
        prefix_inputs=(jnp.full((1,), lam_init, F32),),
        prefix_specs=(pl.BlockSpec(memory_space=pltpu.SMEM),))


def _sb_attention(zqk, zvt):
    return _attention_call(_sb_kernel, "stickbreak", zqk, zvt, MIXER_SB)


def _cast_weights_once(w_ref, w_scr):
    @pl.when(pl.program_id(1) == 0)
    def _():
        w_scr[...] = w_ref[...].astype(BF16)


def _outproj_norm_kernel(a0_ref, a1_ref, a2_ref, a3_ref, w_ref, x_ref, g_ref, o_ref, h_ref):
    tm = x_ref.shape[0]
    for r in range(tm // MM_CHUNK):
        rows = slice(r * MM_CHUNK, (r + 1) * MM_CHUNK)
        heads = [a_ref[0, s, rows, :] for a_ref in (a0_ref, a1_ref, a2_ref, a3_ref)
                 for s in range(HEADS_PER_GROUP)]
        mixed = jnp.concatenate(heads, axis=-1)
        y = x_ref[rows, :] + _dot(mixed, w_ref[...])
        o_ref[rows, :] = y
        h_ref[rows, :] = _rms_normalize(y, g_ref[...], NORM_EPS).astype(BF16)


def _outproj_norm(mixed, w_bf16, layer, x2, g, s_len):
    m, d = x2.shape
    tm = min(NORM_ROWS, s_len)
    assert tm % MM_CHUNK == 0
    npos = s_len // tm
    a_spec = pl.BlockSpec((1, HEADS_PER_GROUP, tm, HEAD_DIM), lambda i: (i // npos, 0, i % npos, 0))
    row_spec = pl.BlockSpec((tm, d), lambda i: (i, 0))
    return pl.pallas_call(
        _outproj_norm_kernel,
        grid=(m // tm,),
        in_specs=[a_spec, a_spec, a_spec, a_spec,
                  pl.BlockSpec((None, d, d), lambda i: (layer, 0, 0)),
                  row_spec,
                  pl.BlockSpec((1, d), lambda i: (0, 0))],
        out_specs=[row_spec, row_spec],
        out_shape=[jax.ShapeDtypeStruct((m, d), F32), jax.ShapeDtypeStruct((m, d), BF16)],
        compiler_params=_params("arbitrary"),
        name="outproj_norm",
    )(*mixed, w_bf16, x2, g)


def _ffn_up_kernel(h_ref, wg_ref, wu_ref, o_ref, wg_scr, wu_scr):
    _cast_weights_once(wg_ref, wg_scr)
    _cast_weights_once(wu_ref, wu_scr)
    h = h_ref[...]
    gate = _dot(h, wg_scr[...])
    up = _dot(h, wu_scr[...])
    o_ref[...] = (gate / (1.0 + jnp.exp(-gate)) * up).astype(BF16)


def _ffn_up(h2, wg, wu, layer, tn=512):
    m, d = h2.shape
    f = wg.shape[2]
    tm = min(MM_ROWS, m)
    w_spec = pl.BlockSpec((None, d, tn), lambda j, i: (layer, 0, j))
    return pl.pallas_call(
        _ffn_up_kernel,
        grid=(f // tn, m // tm),
        in_specs=[pl.BlockSpec((tm, d), lambda j, i: (i, 0)), w_spec, w_spec],
        out_specs=pl.BlockSpec((tm, tn), lambda j, i: (i, j)),
        out_shape=jax.ShapeDtypeStruct((m, f), BF16),
        scratch_shapes=[pltpu.VMEM((d, tn), BF16), pltpu.VMEM((d, tn), BF16)],
        compiler_params=_params("arbitrary", "arbitrary"),
        name="ffn_up",
    )(h2, wg, wu)


def _ffn_down_kernel(a_ref, w_ref, x_ref, o_ref, w_scr):
    _cast_weights_once(w_ref, w_scr)
    o_ref[...] = x_ref[...] + _dot(a_ref[...], w_scr[...])


def _ffn_down(act, w, layer, x2, tn=512):
    m, d = x2.shape
    f = act.shape[1]
    tm = min(DOWN_ROWS, m)
    return pl.pallas_call(
        _ffn_down_kernel,
        grid=(d // tn, m // tm),
        in_specs=[pl.BlockSpec((tm, f), lambda j, i: (i, 0)),
                  pl.BlockSpec((None, f, tn), lambda j, i: (layer, 0, j)),
                  pl.BlockSpec((tm, tn), lambda j, i: (i, j))],
        out_specs=pl.BlockSpec((tm, tn), lambda j, i: (i, j)),
        out_shape=jax.ShapeDtypeStruct((m, d), F32),
        scratch_shapes=[pltpu.VMEM((f, tn), BF16)],
        compiler_params=_params("arbitrary", "arbitrary"),
        name="ffn_down",
    )(act, w, x2)


def _rope_tables(s_len):
    def cos_sin(dim):
        inv = 1.0 / (ROPE_THETA ** (jnp.arange(0, dim, 2, dtype=F32) / dim))
        ang = jnp.arange(s_len, dtype=F32)[:, None] * inv[None, :]
        return jnp.cos(ang), jnp.sin(ang)

    cf, sf = cos_sin(HEAD_DIM)
    ch, sh = cos_sin(DIFF_QK_DIM)
    zh = jnp.zeros_like(sh)
    return (jnp.concatenate([cf, cf], axis=-1),
            jnp.concatenate([-sf, sf], axis=-1),
            jnp.concatenate([ch, ch, ch, ch], axis=-1),
            jnp.concatenate([-sh, zh, -sh, zh], axis=-1),
            jnp.concatenate([zh, sh, zh, sh], axis=-1))


def kernel(x, w_in, b_fgate, w_out, diff_lq1, diff_lk1, diff_lq2, diff_lk2, diff_subln, attn_norm,
           w_gate, w_up, w_down, ffn_norm, final_norm):
    b, s_len, d = x.shape
    depth = w_in.shape[0]
    m = b * s_len
    tables = _rope_tables(s_len)
    b_fg = jnp.pad(b_fgate, ((0, 0), (0, LANES - HEADS_PER_GROUP)))
    w_out_bf16 = w_out.astype(BF16)
    qk_tile = lambda j: j + j // 2
    v_tile = lambda j: 3 * j + 2

    x2 = x.reshape(m, d)
    for l in range(depth):
        h, cq, ck = _norm_gate(x2.reshape(b, s_len, d), attn_norm[l][None, :], w_in, l, b_fg[l][None, :])
        h2 = h.reshape(m, d)
        zqk = _inproj(h2, w_in, l, tables, b, s_len, qk_tile, QK_KINDS)
        zvt = _inproj(h2, w_in, l, (), b, s_len, v_tile, ("transposed",) * N_MIXERS)
        lam_init = 0.8 - 0.6 * math.exp(-0.3 * l)
        fox = _fox_attention(zqk, zvt, cq, ck)
        moba = _moba_attention(zqk, zvt)
        diff = _diff_attention(zqk, zvt, diff_lq1[l][None, :], diff_lk1[l][None, :], diff_lq2[l][None, :],
                               diff_lk2[l][None, :], diff_subln[l][None, :], lam_init)
        sb = _sb_attention(zqk, zvt)
        x2, hf = _outproj_norm((fox, moba, diff, sb), w_out_bf16, l, x2, ffn_norm[l][None, :], s_len)
        act = _ffn_up(hf, w_gate, w_up, l)
        x2 = _ffn_down(act, w_down, l, x2)
    return _norm(x2, final_norm[None, :], F32).reshape(b, s_len, d)
```

```python
import functools
import math

import jax
import jax.numpy as jnp
from jax import lax
from jax.experimental import pallas as pl
from jax.experimental.pallas import tpu as pltpu

F32 = jnp.float32
BF16 = jnp.bfloat16

HEAD_DIM = 128
HEADS_PER_GROUP = 4
GROUP_WIDTH = HEADS_PER_GROUP * HEAD_DIM
N_MIXERS = 4
GATE_TILE = 3
DIFF_QK_DIM = HEAD_DIM // 2
MOBA_BLOCK = 256
MOBA_TOPK = 3
ROPE_THETA = 10000.0
NORM_EPS = 1e-6
DIFF_SUBLN_EPS = 1e-5
LANES = 128
SUBLANES = 8
BF16_ROWS = 16
NEG_INF = float("-inf")
LOG2E = math.log2(math.e)
ATTN_TILE = 256
HEADS_PER_STEP = 2
NORM_ROWS = 512
INPROJ_ROWS = 1024
FFN_UP_ROWS = 2048
MM_CHUNK = 256
FFN_UP_CHUNK = 512
DOWN_ROWS = 512

VMEM_LIMIT_BYTES = 56 * 1024 * 1024

MIXER_FOX, MIXER_MOBA, MIXER_DIFF, MIXER_SB = range(N_MIXERS)


def _nt_dot(a, b):
    return lax.dot_general(a, b, (((1,), (1,)), ((), ())), preferred_element_type=F32)


def _dot(a, b):
    return jnp.dot(a, b, preferred_element_type=F32)


def _split2_bf16(x):
    hi = x.astype(BF16)
    lo = (x - hi.astype(F32)).astype(BF16)
    return hi, lo


def _split3_bf16(x):
    hi = x.astype(BF16)
    r = x - hi.astype(F32)
    mid = r.astype(BF16)
    lo = (r - mid.astype(F32)).astype(BF16)
    return hi, mid, lo


def _rms_normalize(x, g, eps):
    ms = jnp.mean(x * x, axis=-1, keepdims=True)
    return x * lax.rsqrt(ms + eps) * g


def _params(*semantics):
    return pltpu.CompilerParams(dimension_semantics=semantics, vmem_limit_bytes=VMEM_LIMIT_BYTES)


def _norm_kernel(x_ref, g_ref, o_ref):
    o_ref[...] = _rms_normalize(x_ref[...], g_ref[...], NORM_EPS).astype(o_ref.dtype)


def _norm(x2, g, out_dtype):
    m, d = x2.shape
    tm = min(NORM_ROWS, m)
    return pl.pallas_call(
        _norm_kernel,
        grid=(m // tm,),
        in_specs=[pl.BlockSpec((tm, d), lambda i: (i, 0)), pl.BlockSpec((1, d), lambda i: (0, 0))],
        out_specs=pl.BlockSpec((tm, d), lambda i: (i, 0)),
        out_shape=jax.ShapeDtypeStruct((m, d), out_dtype),
        compiler_params=_params("arbitrary"),
        name="norm",
    )(x2, g)


def _norm_gate_kernel(x_ref, g_ref, wfg_ref, b_ref, h_ref, cq_ref, ck_ref, carry_ref):
    tm = x_ref.shape[1]

    @pl.when(pl.program_id(1) == 0)
    def _():
        carry_ref[...] = jnp.zeros_like(carry_ref)

    h = _rms_normalize(x_ref[0], g_ref[...], NORM_EPS).astype(BF16)
    h_ref[0] = h
    f = _dot(h, wfg_ref[...].astype(BF16)) + b_ref[...]
    ls = jnp.minimum(f, 0.0) - jnp.log1p(jnp.exp(-jnp.abs(f)))
    row = lax.broadcasted_iota(jnp.int32, (tm, tm), 0)
    col = lax.broadcasted_iota(jnp.int32, (tm, tm), 1)
    tri = jnp.where(row >= col, 1.0, 0.0).astype(BF16)
    hi, mid, lo = _split3_bf16(ls)
    c = _dot(tri, hi) + _dot(tri, mid) + _dot(tri, lo) + carry_ref[...]
    carry_ref[...] = c[tm - 1:tm, :]

    lane = lax.broadcasted_iota(jnp.int32, (tm, LANES), 1)
    for head in range(HEADS_PER_GROUP):
        parts = _split3_bf16(jnp.broadcast_to(c[:, head:head + 1] * LOG2E, (tm, LANES)))
        n_parts = len(parts)
        q_cols = jnp.where(lane < 2 * n_parts, 1.0, 0.0)
        k_cols = jnp.where(lane < n_parts, 1.0, 0.0)
        for n, part in enumerate(parts):
            q_cols = jnp.where(lane == n, part.astype(F32), q_cols)
            k_cols = jnp.where(lane == n_parts + n, -part.astype(F32), k_cols)
        cq_ref[0, head] = q_cols.astype(BF16)
        ck_ref[0, head] = k_cols.astype(BF16)


def _norm_gate(x3, g, w_in, layer, b_fg):
    b, s_len, d = x3.shape
    tm = min(NORM_ROWS, s_len)
    gate_block = GATE_TILE * GROUP_WIDTH // LANES
    aug_spec = pl.BlockSpec((1, HEADS_PER_GROUP, tm, LANES), lambda bi, i: (bi, 0, i, 0))
    aug_shape = jax.ShapeDtypeStruct((b, HEADS_PER_GROUP, s_len, LANES), BF16)
    return pl.pallas_call(
        _norm_gate_kernel,
        grid=(b, s_len // tm),
        in_specs=[pl.BlockSpec((1, tm, d), lambda bi, i: (bi, i, 0)),
                  pl.BlockSpec((1, d), lambda bi, i: (0, 0)),
                  pl.BlockSpec((None, d, LANES), lambda bi, i: (layer, 0, gate_block)),
                  pl.BlockSpec((1, LANES), lambda bi, i: (0, 0))],
        out_specs=[pl.BlockSpec((1, tm, d), lambda bi, i: (bi, i, 0)), aug_spec, aug_spec],
        out_shape=[jax.ShapeDtypeStruct((b, s_len, d), BF16), aug_shape, aug_shape],
        scratch_shapes=[pltpu.VMEM((1, LANES), F32)],
        compiler_params=_params("arbitrary", "arbitrary"),
        name="norm_gate",
    )(x3, g, w_in, b_fg)


def _inproj_kernel(h_ref, wa_ref, wb_ref, cf_ref, sf_ref, ch_ref, sha_ref, shb_ref, zqk_ref, zvt_ref, w_scr,
                   *, npos):
    mixer = pl.program_id(0)
    i = pl.program_id(1)
    tm = h_ref.shape[0]
    n_slabs = wa_ref.shape[1] // LANES
    shift = HEADS_PER_GROUP

    @pl.when((i == 0) & (mixer == MIXER_FOX))
    def _():
        w_scr[...] = wa_ref[...].astype(BF16)

    @pl.when((i == 0) & (mixer != MIXER_FOX))
    def _():
        lane = lax.broadcasted_iota(jnp.int32, (wa_ref.shape[0], LANES), 1)
        for s in range(n_slabs):
            cur = wa_ref[:, s * LANES:(s + 1) * LANES]
            nxt = wa_ref[:, (s + 1) * LANES:(s + 2) * LANES] if s + 1 < n_slabs else wb_ref[...]
            moved = jnp.where(lane < LANES - shift, pltpu.roll(cur, LANES - shift, 1),
                              pltpu.roll(nxt, LANES - shift, 1))
            w_scr[:, s * LANES:(s + 1) * LANES] = moved.astype(BF16)

    scale_full = HEAD_DIM ** -0.5 * LOG2E
    scale_half = DIFF_QK_DIM ** -0.5 * LOG2E

    def rope_full(a, pos):
        return a * cf_ref[pos, :] + pltpu.roll(a, HEAD_DIM // 2, 1) * sf_ref[pos, :]

    def rope_half(a, pos):
        return (a * ch_ref[pos, :] + pltpu.roll(a, HEAD_DIM - DIFF_QK_DIM // 2, 1) * sha_ref[pos, :]
                + pltpu.roll(a, DIFF_QK_DIM // 2, 1) * shb_ref[pos, :])

    def run(rope, q_scale):
        for r in range(tm // MM_CHUNK):
            rows = slice(r * MM_CHUNK, (r + 1) * MM_CHUNK)
            acc = _dot(h_ref[rows, :], w_scr[...])
            pos = pl.ds(pl.multiple_of((i % npos) * tm + r * MM_CHUNK, MM_CHUNK), MM_CHUNK)
            for s in range(HEADS_PER_GROUP):
                q = acc[:, s * HEAD_DIM:(s + 1) * HEAD_DIM]
                k = acc[:, GROUP_WIDTH + s * HEAD_DIM:GROUP_WIDTH + (s + 1) * HEAD_DIM]
                v = acc[:, 2 * GROUP_WIDTH + s * HEAD_DIM:2 * GROUP_WIDTH + (s + 1) * HEAD_DIM]
                if rope is not None:
                    q, k = rope(q, pos), rope(k, pos)
                zqk_ref[0, s, rows, :] = (q * q_scale).astype(BF16)
                zqk_ref[0, HEADS_PER_GROUP + s, rows, :] = k.astype(BF16)
                zvt_ref[0, s, :, rows] = jnp.transpose(v).astype(BF16)

    pl.when((mixer == MIXER_FOX) | (mixer == MIXER_SB))(functools.partial(run, None, scale_full))
    pl.when(mixer == MIXER_MOBA)(functools.partial(run, rope_full, scale_full))
    pl.when(mixer == MIXER_DIFF)(functools.partial(run, rope_half, scale_half))


def _inproj(h2, w_in, layer, tables, b, s_len):
    m, d = h2.shape
    tm = min(INPROJ_ROWS, s_len)
    assert tm % MM_CHUNK == 0
    npos = s_len // tm
    width = 3 * GROUP_WIDTH
    once = dict(pipeline_mode=pl.Buffered(1))
    tab_spec = pl.BlockSpec((s_len, LANES), lambda j, i: (0, 0), **once)
    return pl.pallas_call(
        functools.partial(_inproj_kernel, npos=npos),
        grid=(N_MIXERS, m // tm),
        in_specs=[
            pl.BlockSpec((tm, d), lambda j, i: (i, 0)),
            pl.BlockSpec((None, d, width), lambda j, i: (layer, 0, j)),
            pl.BlockSpec((None, d, LANES), lambda j, i: (layer, 0, (j + 1) * (width // LANES))),
            tab_spec, tab_spec, tab_spec, tab_spec, tab_spec,
        ],
        out_specs=[
            pl.BlockSpec((1, 2 * HEADS_PER_GROUP, tm, HEAD_DIM), lambda j, i: (i // npos, j, i % npos, 0)),
            pl.BlockSpec((1, HEADS_PER_GROUP, HEAD_DIM, tm), lambda j, i: (i // npos, j, 0, i % npos)),
        ],
        out_shape=[
            jax.ShapeDtypeStruct((b, 2 * N_MIXERS * HEADS_PER_GROUP, s_len, HEAD_DIM), BF16),
            jax.ShapeDtypeStruct((b, N_MIXERS * HEADS_PER_GROUP, HEAD_DIM, s_len), BF16),
        ],
        scratch_shapes=[pltpu.VMEM((d, width), BF16)],
        compiler_params=_params("arbitrary", "arbitrary"),
        name="inproj",
    )(h2, w_in, w_in, *tables)


def _key_query_mask(t, strict=False):
    key = lax.broadcasted_iota(jnp.int32, (t, t), 0)
    query = lax.broadcasted_iota(jnp.int32, (t, t), 1)
    return (key < query) if strict else (key <= query)


def _softmax_weights(logits):
    m = None
    for s in logits:
        mj = jnp.max(s, axis=0, keepdims=True)
        m = mj if m is None else jnp.maximum(m, mj)
    ps = [jnp.exp2(s - m) for s in logits]
    l = None
    for p in ps:
        lj = jnp.sum(p, axis=0, keepdims=True)
        l = lj if l is None else l + lj
    return ps, l


def _weighted_values(ps, vts):
    acc = None
    for p, vt in zip(ps, vts):
        aj = _dot(vt, p.astype(BF16))
        acc = aj if acc is None else acc + aj
    return acc


def _pipelined_tiles(n_tiles, head, tail, depth=1):
    states = {i: head(i) for i in range(min(depth, n_tiles))}
    for i in range(n_tiles):
        if i + depth < n_tiles:
            states[i + depth] = head(i + depth)
        tail(i, states.pop(i))


def _fox_kernel(hh, q_ref, k_ref, vt_ref, cq_ref, ck_ref, o_ref, *, t):
    s_len = q_ref.shape[2]
    diag = _key_query_mask(t)

    def keys(a, b):
        return jnp.concatenate([k_ref[0, hh, a:b, :], ck_ref[0, hh, a:b, :]], axis=1)

    def head(i):
        lo, hi = i * t, (i + 1) * t
        q = jnp.concatenate([q_ref[0, hh, lo:hi, :], cq_ref[0, hh, lo:hi, :]], axis=1)
        logits, vts = [], []
        if i > 0:
            logits.append(_nt_dot(keys(0, lo), q))
            vts.append(vt_ref[0, hh, :, :lo])
        logits.append(jnp.where(diag, _nt_dot(keys(lo, hi), q), NEG_INF))
        vts.append(vt_ref[0, hh, :, lo:hi])
        return logits, vts

    def tail(i, state):
        logits, vts = state
        ps, l = _softmax_weights(logits)
        out_t = _weighted_values(ps, vts) * (1.0 / l)
        o_ref[0, hh, i * t:(i + 1) * t, :] = jnp.transpose(out_t).astype(o_ref.dtype)

    _pipelined_tiles(s_len // t, head, tail, depth=2)


def _moba_kernel(hh, q_ref, k_ref, vt_ref, o_ref, *, t):
    s_len = q_ref.shape[2]

    brow = lax.broadcasted_iota(jnp.int32, (BF16_ROWS, s_len), 0)
    bcol = lax.broadcasted_iota(jnp.int32, (BF16_ROWS, s_len), 1)
    member = jnp.where(bcol // t == brow, 1.0, 0.0).astype(BF16)
    k_mean = _dot(member, k_ref[0, hh]) * (1.0 / t)
    km_parts = _split3_bf16(k_mean)
    block = lax.broadcasted_iota(jnp.int32, (BF16_ROWS, t), 0)
    diag = _key_query_mask(t)

    def head(i):
        lo, hi = i * t, (i + 1) * t
        q = q_ref[0, hh, lo:hi, :]
        s_past = _nt_dot(k_ref[0, hh, :lo, :], q) if i > 0 else None
        s_diag = _nt_dot(k_ref[0, hh, lo:hi, :], q)
        gate = sum(_nt_dot(part, q) for part in km_parts) if i > MOBA_TOPK else None
        return s_past, s_diag, gate

    def tail(i, state):
        lo, hi = i * t, (i + 1) * t
        s_past, s_diag, gate = state
        logits, vts = [], []
        if 0 < i <= MOBA_TOPK:
            logits.append(s_past)
            vts.append(vt_ref[0, hh, :, :lo])
        elif i > MOBA_TOPK:
            gate = jnp.where(block < i, gate, NEG_INF)
            beaten = jnp.zeros((BF16_ROWS, t), F32)
            for mblk in range(i):
                gm = gate[mblk:mblk + 1, :]
                wins = jnp.where(gm > gate, 1.0, jnp.where((gm == gate) & (mblk < block), 1.0, 0.0))
                beaten = beaten + wins
            for n in range(i):
                keep_n = beaten[n:n + 1, :] < MOBA_TOPK
                logits.append(jnp.where(keep_n, s_past[n * t:(n + 1) * t, :], NEG_INF))
                vts.append(vt_ref[0, hh, :, n * t:(n + 1) * t])
        logits.append(jnp.where(diag, s_diag, NEG_INF))
        vts.append(vt_ref[0, hh, :, lo:hi])
        ps, l = _softmax_weights(logits)
        out_t = _weighted_values(ps, vts) * (1.0 / l)
        o_ref[0, hh, lo:hi, :] = jnp.transpose(out_t).astype(o_ref.dtype)

    _pipelined_tiles(s_len // t, head, tail, depth=2)


def _diff_kernel(hh, lam_init_ref, q_ref, k_ref, vt_ref, lq1_ref, lk1_ref, lq2_ref, lk2_ref, gsub_ref, o_ref,
                 *, t):
    s_len = q_ref.shape[2]
    lam_init = lam_init_ref[0]
    lam = (jnp.exp(jnp.sum(lq1_ref[...] * lk1_ref[...], axis=-1, keepdims=True))
           - jnp.exp(jnp.sum(lq2_ref[...] * lk2_ref[...], axis=-1, keepdims=True)) + lam_init)
    lane = lax.broadcasted_iota(jnp.int32, (t, LANES), 1)
    diag = _key_query_mask(t)

    def head(i):
        lo, hi = i * t, (i + 1) * t
        q = q_ref[0, hh, lo:hi, :]
        zero = jnp.zeros_like(q)
        halves = (jnp.where(lane < DIFF_QK_DIM, q, zero), jnp.where(lane >= DIFF_QK_DIM, q, zero))
        both = []
        for qh in halves:
            logits = []
            if i > 0:
                logits.append(_nt_dot(k_ref[0, hh, :lo, :], qh))
            logits.append(jnp.where(diag, _nt_dot(k_ref[0, hh, lo:hi, :], qh), NEG_INF))
            both.append(logits)
        return both

    def tail(i, both):
        lo, hi = i * t, (i + 1) * t
        (ps1, l1), (ps2, l2) = (_softmax_weights(logits) for logits in both)
        ratio = lam * l1 / l2
        ps = [p1 - p2 * ratio for p1, p2 in zip(ps1, ps2)]
        vts = ([vt_ref[0, hh, :, :lo]] if i > 0 else []) + [vt_ref[0, hh, :, lo:hi]]
        out = jnp.transpose(_weighted_values(ps, vts) * (1.0 / l1))
        y = _rms_normalize(out, gsub_ref[...], DIFF_SUBLN_EPS)
        o_ref[0, hh, lo:hi, :] = (y * (1.0 - lam_init)).astype(o_ref.dtype)

    _pipelined_tiles(s_len // t, head, tail)


def _sb_kernel(hh, q_ref, k_ref, vt_ref, o_ref, *, t):
    s_len = q_ref.shape[2]
    row = lax.broadcasted_iota(jnp.int32, (t, t), 0)
    col = lax.broadcasted_iota(jnp.int32, (t, t), 1)
    later = jnp.where(col > row, 1.0, 0.0).astype(BF16)
    strict = _key_query_mask(t, strict=True)

    def head(i):
        hi = (i + 1) * t
        z_all = _nt_dot(k_ref[0, hh, :hi, :], q_ref[0, hh, i * t:hi, :])
        blocks = []
        for n in range(i, -1, -1):
            z = z_all[n * t:(n + 1) * t, :]
            sp = jnp.log2(1.0 + jnp.exp2(-jnp.abs(z)))
            log_beta = jnp.minimum(z, 0.0) - sp
            log_1m = log_beta - z
            if n == i:
                log_1m = jnp.where(strict, log_1m, 0.0)
            l_hi, l_lo = _split2_bf16(log_1m)
            within = _dot(later, l_hi) + _dot(later, l_lo)
            blocks.append((n, log_beta, within, within[0:1, :] + log_1m[0:1, :]))
        return blocks

    def tail(i, blocks):
        run = None
        acc = None
        for n, log_beta, within, total in blocks:
            a = jnp.exp2(log_beta + within)
            if n == i:
                a = jnp.where(strict, a, 0.0)
            local = _dot(vt_ref[0, hh, :, n * t:(n + 1) * t], a.astype(BF16))
            acc = local if acc is None else acc + local * jnp.exp2(run)
            run = total if run is None else run + total
        o_ref[0, hh, i * t:(i + 1) * t, :] = jnp.transpose(acc).astype(o_ref.dtype)

    _pipelined_tiles(s_len // t, head, tail)


def _attention_call(kernel_fn, name, zqk, zvt, mixer, extra_inputs=(), extra_specs=(), prefix_inputs=(),
                    prefix_specs=()):
    b, _, s_len, _ = zqk.shape
    t = min(ATTN_TILE, s_len)
    hq = 2 * mixer * HEADS_PER_GROUP
    hk = hq + HEADS_PER_GROUP
    hv = mixer * HEADS_PER_GROUP
    hs = HEADS_PER_STEP
    seq_spec = lambda h0: pl.BlockSpec((1, hs, s_len, HEAD_DIM), lambda bi, h: (bi, h0 // hs + h, 0, 0))

    def body(*refs):
        for hh in range(hs):
            kernel_fn(hh, *refs, t=t)

    return pl.pallas_call(
        body,
        grid=(b, HEADS_PER_GROUP // hs),
        in_specs=[*prefix_specs, seq_spec(hq), seq_spec(hk),
                  pl.BlockSpec((1, hs, HEAD_DIM, s_len), lambda bi, h: (bi, hv // hs + h, 0, 0)), *extra_specs],
        out_specs=seq_spec(0),
        out_shape=jax.ShapeDtypeStruct((b, HEADS_PER_GROUP, s_len, HEAD_DIM), BF16),
        compiler_params=_params("arbitrary", "arbitrary"),
        name=name,
    )(*prefix_inputs, zqk, zqk, zvt, *extra_inputs)


def _fox_attention(zqk, zvt, cq, ck):
    s_len = zqk.shape[2]
    aug_spec = pl.BlockSpec((1, HEADS_PER_STEP, s_len, LANES), lambda bi, h: (bi, h, 0, 0))
    return _attention_call(_fox_kernel, "fox", zqk, zvt, MIXER_FOX, (cq, ck), (aug_spec, aug_spec))


def _moba_attention(zqk, zvt):
    s_len = zqk.shape[2]
    assert ATTN_TILE == MOBA_BLOCK and s_len % MOBA_BLOCK == 0 and s_len // MOBA_BLOCK <= BF16_ROWS
    return _attention_call(_moba_kernel, "moba", zqk, zvt, MIXER_MOBA)


def _diff_attention(zqk, zvt, lq1, lk1, lq2, lk2, gsub, lam_init):
    vec = lambda n: pl.BlockSpec((1, n), lambda bi, h: (0, 0))
    return _attention_call(
        _diff_kernel, "diff", zqk, zvt, MIXER_DIFF, (lq1, lk1, lq2, lk2, gsub),
        (vec(DIFF_QK_DIM), vec(DIFF_QK_DIM), vec(DIFF_QK_DIM), vec(DIFF_QK_DIM), vec(HEAD_DIM)),
        prefix_inputs=(jnp.full((1,), lam_init, F32),),
        prefix_specs=(pl.BlockSpec(memory_space=pltpu.SMEM),))


def _sb_attention(zqk, zvt):
    return _attention_call(_sb_kernel, "stickbreak", zqk, zvt, MIXER_SB)


def _cast_weights_once(w_ref, w_scr):
    @pl.when(pl.program_id(1) == 0)
    def _():
        w_scr[...] = w_ref[...].astype(BF16)


def _outproj_norm_kernel(a0_ref, a1_ref, a2_ref, a3_ref, w_ref, x_ref, g_ref, o_ref, h_ref):
    tm = x_ref.shape[0]
    for r in range(tm // MM_CHUNK):
        rows = slice(r * MM_CHUNK, (r + 1) * MM_CHUNK)
        heads = [a_ref[0, s, rows, :] for a_ref in (a0_ref, a1_ref, a2_ref, a3_ref)
                 for s in range(HEADS_PER_GROUP)]
        mixed = jnp.concatenate(heads, axis=-1)
        y = x_ref[rows, :] + _dot(mixed, w_ref[...])
        o_ref[rows, :] = y
        h_ref[rows, :] = _rms_normalize(y, g_ref[...], NORM_EPS).astype(BF16)


def _outproj_norm(mixed, w_bf16, layer, x2, g, s_len):
    m, d = x2.shape
    tm = min(NORM_ROWS, s_len)
    assert tm % MM_CHUNK == 0
    npos = s_len // tm
    a_spec = pl.BlockSpec((1, HEADS_PER_GROUP, tm, HEAD_DIM), lambda i: (i // npos, 0, i % npos, 0))
    row_spec = pl.BlockSpec((tm, d), lambda i: (i, 0))
    return pl.pallas_call(
        _outproj_norm_kernel,
        grid=(m // tm,),
        in_specs=[a_spec, a_spec, a_spec, a_spec,
                  pl.BlockSpec((None, d, d), lambda i: (layer, 0, 0)),
                  row_spec,
                  pl.BlockSpec((1, d), lambda i: (0, 0))],
        out_specs=[row_spec, row_spec],
        out_shape=[jax.ShapeDtypeStruct((m, d), F32), jax.ShapeDtypeStruct((m, d), BF16)],
        compiler_params=_params("arbitrary"),
        name="outproj_norm",
    )(*mixed, w_bf16, x2, g)


def _ffn_up_kernel(h_ref, wg_ref, wu_ref, o_ref, wg_scr, wu_scr):
    _cast_weights_once(wg_ref, wg_scr)
    _cast_weights_once(wu_ref, wu_scr)
    chunk = min(FFN_UP_CHUNK, h_ref.shape[0])
    for r in range(h_ref.shape[0] // chunk):
        rows = slice(r * chunk, (r + 1) * chunk)
        h = h_ref[rows, :]
        gate = _dot(h, wg_scr[...])
        up = _dot(h, wu_scr[...])
        o_ref[rows, :] = (gate / (1.0 + jnp.exp(-gate)) * up).astype(BF16)


def _ffn_up(h2, wg, wu, layer, tn=512):
    m, d = h2.shape
    f = wg.shape[2]
    tm = min(FFN_UP_ROWS, m)
    w_spec = pl.BlockSpec((None, d, tn), lambda j, i: (layer, 0, j))
    return pl.pallas_call(
        _ffn_up_kernel,
        grid=(f // tn, m // tm),
        in_specs=[pl.BlockSpec((tm, d), lambda j, i: (i, 0)), w_spec, w_spec],
        out_specs=pl.BlockSpec((tm, tn), lambda j, i: (i, j)),
        out_shape=jax.ShapeDtypeStruct((m, f), BF16),
        scratch_shapes=[pltpu.VMEM((d, tn), BF16), pltpu.VMEM((d, tn), BF16)],
        compiler_params=_params("arbitrary", "arbitrary"),
        name="ffn_up",
    )(h2, wg, wu)


def _ffn_down_kernel(a_ref, w_ref, x_ref, o_ref, w_scr):
    _cast_weights_once(w_ref, w_scr)
    o_ref[...] = x_ref[...] + _dot(a_ref[...], w_scr[...])


def _ffn_down(act, w, layer, x2, tn=512):
    m, d = x2.shape
    f = act.shape[1]
    tm = min(DOWN_ROWS, m)
    return pl.pallas_call(
        _ffn_down_kernel,
        grid=(d // tn, m // tm),
        in_specs=[pl.BlockSpec((tm, f), lambda j, i: (i, 0)),
                  pl.BlockSpec((None, f, tn), lambda j, i: (layer, 0, j)),
                  pl.BlockSpec((tm, tn), lambda j, i: (i, j))],
        out_specs=pl.BlockSpec((tm, tn), lambda j, i: (i, j)),
        out_shape=jax.ShapeDtypeStruct((m, d), F32),
        scratch_shapes=[pltpu.VMEM((f, tn), BF16)],
        compiler_params=_params("arbitrary", "arbitrary"),
        name="ffn_down",
    )(act, w, x2)


def _rope_tables(s_len):
    def cos_sin(dim):
        inv = 1.0 / (ROPE_THETA ** (jnp.arange(0, dim, 2, dtype=F32) / dim))
        ang = jnp.arange(s_len, dtype=F32)[:, None] * inv[None, :]
        return jnp.cos(ang), jnp.sin(ang)

    cf, sf = cos_sin(HEAD_DIM)
    ch, sh = cos_sin(DIFF_QK_DIM)
    zh = jnp.zeros_like(sh)
    return (jnp.concatenate([cf, cf], axis=-1),
            jnp.concatenate([-sf, sf], axis=-1),
            jnp.concatenate([ch, ch, ch, ch], axis=-1),
            jnp.concatenate([-sh, zh, -sh, zh], axis=-1),
            jnp.concatenate([zh, sh, zh, sh], axis=-1))


def kernel(x, w_in, b_fgate, w_out, diff_lq1, diff_lk1, diff_lq2, diff_lk2, diff_subln, attn_norm,
           w_gate, w_up, w_down, ffn_norm, final_norm):
    b, s_len, d = x.shape
    depth = w_in.shape[0]
    m = b * s_len
    tables = _rope_tables(s_len)
    b_fg = jnp.pad(b_fgate, ((0, 0), (0, LANES - HEADS_PER_GROUP)))
    w_out_bf16 = w_out.astype(BF16)

    x2 = x.reshape(m, d)
    for l in range(depth):
        h, cq, ck = _norm_gate(x2.reshape(b, s_len, d), attn_norm[l][None, :], w_in, l, b_fg[l][None, :])
        h2 = h.reshape(m, d)
        zqk, zvt = _inproj(h2, w_in, l, tables, b, s_len)
        lam_init = 0.8 - 0.6 * math.exp(-0.3 * l)
        fox = _fox_attention(zqk, zvt, cq, ck)
        moba = _moba_attention(zqk, zvt)
        diff = _diff_attention(zqk, zvt, diff_lq1[l][None, :], diff_lk1[l][None, :], diff_lq2[l][None, :],
                               diff_lk2[l][None, :], diff_subln[l][None, :], lam_init)
        sb = _sb_attention(zqk, zvt)
        x2, hf = _outproj_norm((fox, moba, diff, sb), w_out_bf16, l, x2, ffn_norm[l][None, :], s_len)
        act = _ffn_up(hf, w_gate, w_up, l)
        x2 = _ffn_down(act, w_down, l, x2)
    return _norm(x2, final_norm[None, :], F32).reshape(b, s_len, d)
```

```python
import functools
import math

import jax
import jax.numpy as jnp
from jax import lax
from jax.experimental import pallas as pl
from jax.experimental.pallas import tpu as pltpu

F32 = jnp.float32
BF16 = jnp.bfloat16

HEAD_DIM = 128
HEADS_PER_GROUP = 4
GROUP_WIDTH = HEADS_PER_GROUP * HEAD_DIM
N_MIXERS = 4
GATE_TILE = 3
DIFF_QK_DIM = HEAD_DIM // 2
MOBA_BLOCK = 256
MOBA_TOPK = 3
ROPE_THETA = 10000.0
NORM_EPS = 1e-6
DIFF_SUBLN_EPS = 1e-5
LANES = 128
SUBLANES = 8
BF16_ROWS = 16
NEG_INF = float("-inf")
LOG2E = math.log2(math.e)
ATTN_TILE = 256
HEADS_PER_STEP = 4
NORM_ROWS = 512
INPROJ_ROWS = 1024
FFN_UP_ROWS = 2048
MM_CHUNK = 256
FFN_UP_CHUNK = 512
DOWN_ROWS = 512

VMEM_LIMIT_BYTES = 56 * 1024 * 1024

MIXER_FOX, MIXER_MOBA, MIXER_DIFF, MIXER_SB = range(N_MIXERS)


def _nt_dot(a, b):
    return lax.dot_general(a, b, (((1,), (1,)), ((), ())), preferred_element_type=F32)


def _dot(a, b):
    return jnp.dot(a, b, preferred_element_type=F32)


def _split2_bf16(x):
    hi = x.astype(BF16)
    lo = (x - hi.astype(F32)).astype(BF16)
    return hi, lo


def _split3_bf16(x):
    hi = x.astype(BF16)
    r = x - hi.astype(F32)
    mid = r.astype(BF16)
    lo = (r - mid.astype(F32)).astype(BF16)
    return hi, mid, lo


def _rms_normalize(x, g, eps):
    ms = jnp.mean(x * x, axis=-1, keepdims=True)
    return x * lax.rsqrt(ms + eps) * g


def _params(*semantics):
    return pltpu.CompilerParams(dimension_semantics=semantics, vmem_limit_bytes=VMEM_LIMIT_BYTES)


def _norm_kernel(x_ref, g_ref, o_ref):
    o_ref[...] = _rms_normalize(x_ref[...], g_ref[...], NORM_EPS).astype(o_ref.dtype)


def _norm(x2, g, out_dtype):
    m, d = x2.shape
    tm = min(NORM_ROWS, m)
    return pl.pallas_call(
        _norm_kernel,
        grid=(m // tm,),
        in_specs=[pl.BlockSpec((tm, d), lambda i: (i, 0)), pl.BlockSpec((1, d), lambda i: (0, 0))],
        out_specs=pl.BlockSpec((tm, d), lambda i: (i, 0)),
        out_shape=jax.ShapeDtypeStruct((m, d), out_dtype),
        compiler_params=_params("arbitrary"),
        name="norm",
    )(x2, g)


def _norm_gate_kernel(x_ref, g_ref, wfg_ref, b_ref, h_ref, cq_ref, ck_ref, carry_ref):
    tm = x_ref.shape[1]

    @pl.when(pl.program_id(1) == 0)
    def _():
        carry_ref[...] = jnp.zeros_like(carry_ref)

    h = _rms_normalize(x_ref[0], g_ref[...], NORM_EPS).astype(BF16)
    h_ref[0] = h
    f = _dot(h, wfg_ref[...].astype(BF16)) + b_ref[...]
    ls = jnp.minimum(f, 0.0) - jnp.log1p(jnp.exp(-jnp.abs(f)))
    row = lax.broadcasted_iota(jnp.int32, (tm, tm), 0)
    col = lax.broadcasted_iota(jnp.int32, (tm, tm), 1)
    tri = jnp.where(row >= col, 1.0, 0.0).astype(BF16)
    hi, mid, lo = _split3_bf16(ls)
    c = _dot(tri, hi) + _dot(tri, mid) + _dot(tri, lo) + carry_ref[...]
    carry_ref[...] = c[tm - 1:tm, :]

    lane = lax.broadcasted_iota(jnp.int32, (tm, LANES), 1)
    for head in range(HEADS_PER_GROUP):
        parts = _split3_bf16(jnp.broadcast_to(c[:, head:head + 1] * LOG2E, (tm, LANES)))
        n_parts = len(parts)
        q_cols = jnp.where(lane < 2 * n_parts, 1.0, 0.0)
        k_cols = jnp.where(lane < n_parts, 1.0, 0.0)
        for n, part in enumerate(parts):
            q_cols = jnp.where(lane == n, part.astype(F32), q_cols)
            k_cols = jnp.where(lane == n_parts + n, -part.astype(F32), k_cols)
        cq_ref[0, head] = q_cols.astype(BF16)
        ck_ref[0, head] = k_cols.astype(BF16)


def _norm_gate(x3, g, w_in, layer, b_fg):
    b, s_len, d = x3.shape
    tm = min(NORM_ROWS, s_len)
    gate_block = GATE_TILE * GROUP_WIDTH // LANES
    aug_spec = pl.BlockSpec((1, HEADS_PER_GROUP, tm, LANES), lambda bi, i: (bi, 0, i, 0))
    aug_shape = jax.ShapeDtypeStruct((b, HEADS_PER_GROUP, s_len, LANES), BF16)
    return pl.pallas_call(
        _norm_gate_kernel,
        grid=(b, s_len // tm),
        in_specs=[pl.BlockSpec((1, tm, d), lambda bi, i: (bi, i, 0)),
                  pl.BlockSpec((1, d), lambda bi, i: (0, 0)),
                  pl.BlockSpec((None, d, LANES), lambda bi, i: (layer, 0, gate_block)),
                  pl.BlockSpec((1, LANES), lambda bi, i: (0, 0))],
        out_specs=[pl.BlockSpec((1, tm, d), lambda bi, i: (bi, i, 0)), aug_spec, aug_spec],
        out_shape=[jax.ShapeDtypeStruct((b, s_len, d), BF16), aug_shape, aug_shape],
        scratch_shapes=[pltpu.VMEM((1, LANES), F32)],
        compiler_params=_params("arbitrary", "arbitrary"),
        name="norm_gate",
    )(x3, g, w_in, b_fg)


def _inproj_kernel(h_ref, wa_ref, wb_ref, cf_ref, sf_ref, ch_ref, sha_ref, shb_ref, zqk_ref, zvt_ref, w_scr,
                   *, npos):
    mixer = pl.program_id(0)
    i = pl.program_id(1)
    tm = h_ref.shape[0]
    n_slabs = wa_ref.shape[1] // LANES
    shift = HEADS_PER_GROUP

    @pl.when((i == 0) & (mixer == MIXER_FOX))
    def _():
        w_scr[...] = wa_ref[...].astype(BF16)

    @pl.when((i == 0) & (mixer != MIXER_FOX))
    def _():
        lane = lax.broadcasted_iota(jnp.int32, (wa_ref.shape[0], LANES), 1)
        for s in range(n_slabs):
            cur = wa_ref[:, s * LANES:(s + 1) * LANES]
            nxt = wa_ref[:, (s + 1) * LANES:(s + 2) * LANES] if s + 1 < n_slabs else wb_ref[...]
            moved = jnp.where(lane < LANES - shift, pltpu.roll(cur, LANES - shift, 1),
                              pltpu.roll(nxt, LANES - shift, 1))
            w_scr[:, s * LANES:(s + 1) * LANES] = moved.astype(BF16)

    scale_full = HEAD_DIM ** -0.5 * LOG2E
    scale_half = DIFF_QK_DIM ** -0.5 * LOG2E

    def rope_full(a, pos):
        return a * cf_ref[pos, :] + pltpu.roll(a, HEAD_DIM // 2, 1) * sf_ref[pos, :]

    def rope_half(a, pos):
        return (a * ch_ref[pos, :] + pltpu.roll(a, HEAD_DIM - DIFF_QK_DIM // 2, 1) * sha_ref[pos, :]
                + pltpu.roll(a, DIFF_QK_DIM // 2, 1) * shb_ref[pos, :])

    def run(rope, q_scale):
        for r in range(tm // MM_CHUNK):
            rows = slice(r * MM_CHUNK, (r + 1) * MM_CHUNK)
            acc = _dot(h_ref[rows, :], w_scr[...])
            pos = pl.ds(pl.multiple_of((i % npos) * tm + r * MM_CHUNK, MM_CHUNK), MM_CHUNK)
            for s in range(HEADS_PER_GROUP):
                q = acc[:, s * HEAD_DIM:(s + 1) * HEAD_DIM]
                k = acc[:, GROUP_WIDTH + s * HEAD_DIM:GROUP_WIDTH + (s + 1) * HEAD_DIM]
                v = acc[:, 2 * GROUP_WIDTH + s * HEAD_DIM:2 * GROUP_WIDTH + (s + 1) * HEAD_DIM]
                if rope is not None:
                    q, k = rope(q, pos), rope(k, pos)
                zqk_ref[0, s, rows, :] = (q * q_scale).astype(BF16)
                zqk_ref[0, HEADS_PER_GROUP + s, rows, :] = k.astype(BF16)
                zvt_ref[0, s, :, rows] = jnp.transpose(v).astype(BF16)

    pl.when((mixer == MIXER_FOX) | (mixer == MIXER_SB))(functools.partial(run, None, scale_full))
    pl.when(mixer == MIXER_MOBA)(functools.partial(run, rope_full, scale_full))
    pl.when(mixer == MIXER_DIFF)(functools.partial(run, rope_half, scale_half))


def _inproj(h2, w_in, layer, tables, b, s_len):
    m, d = h2.shape
    tm = min(INPROJ_ROWS, s_len)
    assert tm % MM_CHUNK == 0
    npos = s_len // tm
    width = 3 * GROUP_WIDTH
    once = dict(pipeline_mode=pl.Buffered(1))
    tab_spec = pl.BlockSpec((s_len, LANES), lambda j, i: (0, 0), **once)
    return pl.pallas_call(
        functools.partial(_inproj_kernel, npos=npos),
        grid=(N_MIXERS, m // tm),
        in_specs=[
            pl.BlockSpec((tm, d), lambda j, i: (i, 0)),
            pl.BlockSpec((None, d, width), lambda j, i: (layer, 0, j)),
            pl.BlockSpec((None, d, LANES), lambda j, i: (layer, 0, (j + 1) * (width // LANES))),
            tab_spec, tab_spec, tab_spec, tab_spec, tab_spec,
        ],
        out_specs=[
            pl.BlockSpec((1, 2 * HEADS_PER_GROUP, tm, HEAD_DIM), lambda j, i: (i // npos, j, i % npos, 0)),
            pl.BlockSpec((1, HEADS_PER_GROUP, HEAD_DIM, tm), lambda j, i: (i // npos, j, 0, i % npos)),
        ],
        out_shape=[
            jax.ShapeDtypeStruct((b, 2 * N_MIXERS * HEADS_PER_GROUP, s_len, HEAD_DIM), BF16),
            jax.ShapeDtypeStruct((b, N_MIXERS * HEADS_PER_GROUP, HEAD_DIM, s_len), BF16),
        ],
        scratch_shapes=[pltpu.VMEM((d, width), BF16)],
        compiler_params=_params("arbitrary", "arbitrary"),
        name="inproj",
    )(h2, w_in, w_in, *tables)


def _key_query_mask(t, strict=False):
    key = lax.broadcasted_iota(jnp.int32, (t, t), 0)
    query = lax.broadcasted_iota(jnp.int32, (t, t), 1)
    return (key < query) if strict else (key <= query)


def _softmax_weights(logits):
    m = None
    for s in logits:
        mj = jnp.max(s, axis=0, keepdims=True)
        m = mj if m is None else jnp.maximum(m, mj)
    ps = [jnp.exp2(s - m) for s in logits]
    l = None
    for p in ps:
        lj = jnp.sum(p, axis=0, keepdims=True)
        l = lj if l is None else l + lj
    return ps, l


def _weighted_values(ps, vts):
    acc = None
    for p, vt in zip(ps, vts):
        aj = _dot(vt, p.astype(BF16))
        acc = aj if acc is None else acc + aj
    return acc


def _pipelined_tiles(n_tiles, head, tail, depth=1):
    states = {i: head(i) for i in range(min(depth, n_tiles))}
    for i in range(n_tiles):
        if i + depth < n_tiles:
            states[i + depth] = head(i + depth)
        tail(i, states.pop(i))


def _fox_kernel(hh, q_ref, k_ref, vt_ref, cq_ref, ck_ref, o_ref, *, t):
    s_len = q_ref.shape[2]
    diag = _key_query_mask(t)

    def keys(a, b):
        return jnp.concatenate([k_ref[0, hh, a:b, :], ck_ref[0, hh, a:b, :]], axis=1)

    def head(i):
        lo, hi = i * t, (i + 1) * t
        q = jnp.concatenate([q_ref[0, hh, lo:hi, :], cq_ref[0, hh, lo:hi, :]], axis=1)
        logits, vts = [], []
        if i > 0:
            logits.append(_nt_dot(keys(0, lo), q))
            vts.append(vt_ref[0, hh, :, :lo])
        logits.append(jnp.where(diag, _nt_dot(keys(lo, hi), q), NEG_INF))
        vts.append(vt_ref[0, hh, :, lo:hi])
        return logits, vts

    def tail(i, state):
        logits, vts = state
        ps, l = _softmax_weights(logits)
        out_t = _weighted_values(ps, vts) * (1.0 / l)
        o_ref[0, hh, i * t:(i + 1) * t, :] = jnp.transpose(out_t).astype(o_ref.dtype)

    _pipelined_tiles(s_len // t, head, tail, depth=2)


def _moba_kernel(hh, q_ref, k_ref, vt_ref, o_ref, *, t):
    s_len = q_ref.shape[2]

    brow = lax.broadcasted_iota(jnp.int32, (BF16_ROWS, s_len), 0)
    bcol = lax.broadcasted_iota(jnp.int32, (BF16_ROWS, s_len), 1)
    member = jnp.where(bcol // t == brow, 1.0, 0.0).astype(BF16)
    k_mean = _dot(member, k_ref[0, hh]) * (1.0 / t)
    km_parts = _split3_bf16(k_mean)
    block = lax.broadcasted_iota(jnp.int32, (BF16_ROWS, t), 0)
    diag = _key_query_mask(t)

    def head(i):
        lo, hi = i * t, (i + 1) * t
        q = q_ref[0, hh, lo:hi, :]
        s_past = _nt_dot(k_ref[0, hh, :lo, :], q) if i > 0 else None
        s_diag = _nt_dot(k_ref[0, hh, lo:hi, :], q)
        gate = sum(_nt_dot(part, q) for part in km_parts) if i > MOBA_TOPK else None
        return s_past, s_diag, gate

    def tail(i, state):
        lo, hi = i * t, (i + 1) * t
        s_past, s_diag, gate = state
        logits, vts = [], []
        if 0 < i <= MOBA_TOPK:
            logits.append(s_past)
            vts.append(vt_ref[0, hh, :, :lo])
        elif i > MOBA_TOPK:
            gate = jnp.where(block < i, gate, NEG_INF)
            beaten = jnp.zeros((BF16_ROWS, t), F32)
            for mblk in range(i):
                gm = gate[mblk:mblk + 1, :]
                wins = jnp.where(gm > gate, 1.0, jnp.where((gm == gate) & (mblk < block), 1.0, 0.0))
                beaten = beaten + wins
            for n in range(i):
                keep_n = beaten[n:n + 1, :] < MOBA_TOPK
                logits.append(jnp.where(keep_n, s_past[n * t:(n + 1) * t, :], NEG_INF))
                vts.append(vt_ref[0, hh, :, n * t:(n + 1) * t])
        logits.append(jnp.where(diag, s_diag, NEG_INF))
        vts.append(vt_ref[0, hh, :, lo:hi])
        ps, l = _softmax_weights(logits)
        out_t = _weighted_values(ps, vts) * (1.0 / l)
        o_ref[0, hh, lo:hi, :] = jnp.transpose(out_t).astype(o_ref.dtype)

    _pipelined_tiles(s_len // t, head, tail, depth=2)


def _diff_kernel(hh, lam_init_ref, q_ref, k_ref, vt_ref, lq1_ref, lk1_ref, lq2_ref, lk2_ref, gsub_ref, o_ref,
                 *, t):
    s_len = q_ref.shape[2]
    lam_init = lam_init_ref[0]
    lam = (jnp.exp(jnp.sum(lq1_ref[...] * lk1_ref[...], axis=-1, keepdims=True))
           - jnp.exp(jnp.sum(lq2_ref[...] * lk2_ref[...], axis=-1, keepdims=True)) + lam_init)
    lane = lax.broadcasted_iota(jnp.int32, (t, LANES), 1)
    diag = _key_query_mask(t)

    def head(i):
        lo, hi = i * t, (i + 1) * t
        q = q_ref[0, hh, lo:hi, :]
        zero = jnp.zeros_like(q)
        halves = (jnp.where(lane < DIFF_QK_DIM, q, zero), jnp.where(lane >= DIFF_QK_DIM, q, zero))
        both = []
        for qh in halves:
            logits = []
            if i > 0:
                logits.append(_nt_dot(k_ref[0, hh, :lo, :], qh))
            logits.append(jnp.where(diag, _nt_dot(k_ref[0, hh, lo:hi, :], qh), NEG_INF))
            both.append(logits)
        return both

    def tail(i, both):
        lo, hi = i * t, (i + 1) * t
        (ps1, l1), (ps2, l2) = (_softmax_weights(logits) for logits in both)
        ratio = lam * l1 / l2
        ps = [p1 - p2 * ratio for p1, p2 in zip(ps1, ps2)]
        vts = ([vt_ref[0, hh, :, :lo]] if i > 0 else []) + [vt_ref[0, hh, :, lo:hi]]
        out = jnp.transpose(_weighted_values(ps, vts) * (1.0 / l1))
        y = _rms_normalize(out, gsub_ref[...], DIFF_SUBLN_EPS)
        o_ref[0, hh, lo:hi, :] = (y * (1.0 - lam_init)).astype(o_ref.dtype)

    _pipelined_tiles(s_len // t, head, tail)


def _sb_kernel(hh, q_ref, k_ref, vt_ref, o_ref, *, t):
    s_len = q_ref.shape[2]
    row = lax.broadcasted_iota(jnp.int32, (t, t), 0)
    col = lax.broadcasted_iota(jnp.int32, (t, t), 1)
    later = jnp.where(col > row, 1.0, 0.0).astype(BF16)
    strict = _key_query_mask(t, strict=True)

    def head(i):
        hi = (i + 1) * t
        z_all = _nt_dot(k_ref[0, hh, :hi, :], q_ref[0, hh, i * t:hi, :])
        blocks = []
        for n in range(i, -1, -1):
            z = z_all[n * t:(n + 1) * t, :]
            sp = jnp.log2(1.0 + jnp.exp2(-jnp.abs(z)))
            log_beta = jnp.minimum(z, 0.0) - sp
            log_1m = log_beta - z
            if n == i:
                log_1m = jnp.where(strict, log_1m, 0.0)
            l_hi, l_lo = _split2_bf16(log_1m)
            within = _dot(later, l_hi) + _dot(later, l_lo)
            blocks.append((n, log_beta, within, within[0:1, :] + log_1m[0:1, :]))
        return blocks

    def tail(i, blocks):
        run = None
        acc = None
        for n, log_beta, within, total in blocks:
            a = jnp.exp2(log_beta + within)
            if n == i:
                a = jnp.where(strict, a, 0.0)
            local = _dot(vt_ref[0, hh, :, n * t:(n + 1) * t], a.astype(BF16))
            acc = local if acc is None else acc + local * jnp.exp2(run)
            run = total if run is None else run + total
        o_ref[0, hh, i * t:(i + 1) * t, :] = jnp.transpose(acc).astype(o_ref.dtype)

    _pipelined_tiles(s_len // t, head, tail)


def _attention_call(kernel_fn, name, zqk, zvt, mixer, extra_inputs=(), extra_specs=(), prefix_inputs=(),
                    prefix_specs=()):
    b, _, s_len, _ = zqk.shape
    t = min(ATTN_TILE, s_len)
    hq = 2 * mixer * HEADS_PER_GROUP
    hk = hq + HEADS_PER_GROUP
    hv = mixer * HEADS_PER_GROUP
    hs = HEADS_PER_STEP
    seq_spec = lambda h0: pl.BlockSpec((1, hs, s_len, HEAD_DIM), lambda bi, h: (bi, h0 // hs + h, 0, 0))

    def body(*refs):
        for hh in range(hs):
            kernel_fn(hh, *refs, t=t)

    return pl.pallas_call(
        body,
        grid=(b, HEADS_PER_GROUP // hs),
        in_specs=[*prefix_specs, seq_spec(hq), seq_spec(hk),
                  pl.BlockSpec((1, hs, HEAD_DIM, s_len), lambda bi, h: (bi, hv // hs + h, 0, 0)), *extra_specs],
        out_specs=seq_spec(0),
        out_shape=jax.ShapeDtypeStruct((b, HEADS_PER_GROUP, s_len, HEAD_DIM), BF16),
        compiler_params=_params("arbitrary", "arbitrary"),
        name=name,
    )(*prefix_inputs, zqk, zqk, zvt, *extra_inputs)


def _fox_attention(zqk, zvt, cq, ck):
    s_len = zqk.shape[2]
    aug_spec = pl.BlockSpec((1, HEADS_PER_STEP, s_len, LANES), lambda bi, h: (bi, h, 0, 0))
    return _attention_call(_fox_kernel, "fox", zqk, zvt, MIXER_FOX, (cq, ck), (aug_spec, aug_spec))


def _moba_attention(zqk, zvt):
    s_len = zqk.shape[2]
    assert ATTN_TILE == MOBA_BLOCK and s_len % MOBA_BLOCK == 0 and s_len // MOBA_BLOCK <= BF16_ROWS
    return _attention_call(_moba_kernel, "moba", zqk, zvt, MIXER_MOBA)


def _diff_attention(zqk, zvt, lq1, lk1, lq2, lk2, gsub, lam_init):
    vec = lambda n: pl.BlockSpec((1, n), lambda bi, h: (0, 0))
    return _attention_call(
        _diff_kernel, "diff", zqk, zvt, MIXER_DIFF, (lq1, lk1, lq2, lk2, gsub),
        (vec(DIFF_QK_DIM), vec(DIFF_QK_DIM), vec(DIFF_QK_DIM), vec(DIFF_QK_DIM), vec(HEAD_DIM)),
        prefix_inputs=(jnp.full((1,), lam_init, F32),),
        prefix_specs=(pl.BlockSpec(memory_space=pltpu.SMEM),))


def _sb_attention(zqk, zvt):
    return _attention_call(_sb_kernel, "stickbreak", zqk, zvt, MIXER_SB)


def _cast_weights_once(w_ref, w_scr):
    @pl.when(pl.program_id(1) == 0)
    def _():
        w_scr[...] = w_ref[...].astype(BF16)


def _outproj_norm_kernel(a0_ref, a1_ref, a2_ref, a3_ref, w_ref, x_ref, g_ref, o_ref, h_ref):
    tm = x_ref.shape[0]
    for r in range(tm // MM_CHUNK):
        rows = slice(r * MM_CHUNK, (r + 1) * MM_CHUNK)
        heads = [a_ref[0, s, rows, :] for a_ref in (a0_ref, a1_ref, a2_ref, a3_ref)
                 for s in range(HEADS_PER_GROUP)]
        mixed = jnp.concatenate(heads, axis=-1)
        y = x_ref[rows, :] + _dot(mixed, w_ref[...])
        o_ref[rows, :] = y
        h_ref[rows, :] = _rms_normalize(y, g_ref[...], NORM_EPS).astype(BF16)


def _outproj_norm(mixed, w_bf16, layer, x2, g, s_len):
    m, d = x2.shape
    tm = min(NORM_ROWS, s_len)
    assert tm % MM_CHUNK == 0
    npos = s_len // tm
    a_spec = pl.BlockSpec((1, HEADS_PER_GROUP, tm, HEAD_DIM), lambda i: (i // npos, 0, i % npos, 0))
    row_spec = pl.BlockSpec((tm, d), lambda i: (i, 0))
    return pl.pallas_call(
        _outproj_norm_kernel,
        grid=(m // tm,),
        in_specs=[a_spec, a_spec, a_spec, a_spec,
                  pl.BlockSpec((None, d, d), lambda i: (layer, 0, 0)),
                  row_spec,
                  pl.BlockSpec((1, d), lambda i: (0, 0))],
        out_specs=[row_spec, row_spec],
        out_shape=[jax.ShapeDtypeStruct((m, d), F32), jax.ShapeDtypeStruct((m, d), BF16)],
        compiler_params=_params("arbitrary"),
        name="outproj_norm",
    )(*mixed, w_bf16, x2, g)


def _ffn_up_kernel(h_ref, wg_ref, wu_ref, o_ref, wg_scr, wu_scr):
    _cast_weights_once(wg_ref, wg_scr)
    _cast_weights_once(wu_ref, wu_scr)
    chunk = min(FFN_UP_CHUNK, h_ref.shape[0])
    for r in range(h_ref.shape[0] // chunk):
        rows = slice(r * chunk, (r + 1) * chunk)
        h = h_ref[rows, :]
        gate = _dot(h, wg_scr[...])
        up = _dot(h, wu_scr[...])
        o_ref[rows, :] = (gate / (1.0 + jnp.exp(-gate)) * up).astype(BF16)


def _ffn_up(h2, wg, wu, layer, tn=512):
    m, d = h2.shape
    f = wg.shape[2]
    tm = min(FFN_UP_ROWS, m)
    w_spec = pl.BlockSpec((None, d, tn), lambda j, i: (layer, 0, j))
    return pl.pallas_call(
        _ffn_up_kernel,
        grid=(f // tn, m // tm),
        in_specs=[pl.BlockSpec((tm, d), lambda j, i: (i, 0)), w_spec, w_spec],
        out_specs=pl.BlockSpec((tm, tn), lambda j, i: (i, j)),
        out_shape=jax.ShapeDtypeStruct((m, f), BF16),
        scratch_shapes=[pltpu.VMEM((d, tn), BF16), pltpu.VMEM((d, tn), BF16)],
        compiler_params=_params("arbitrary", "arbitrary"),
        name="ffn_up",
    )(h2, wg, wu)


def _ffn_down_kernel(a_ref, w_ref, x_ref, o_ref, w_scr):
    _cast_weights_once(w_ref, w_scr)
    o_ref[...] = x_ref[...] + _dot(a_ref[...], w_scr[...])


def _ffn_down(act, w, layer, x2, tn=1024):
    m, d = x2.shape
    f = act.shape[1]
    tm = min(DOWN_ROWS, m)
    return pl.pallas_call(
        _ffn_down_kernel,
        grid=(d // tn, m // tm),
        in_specs=[pl.BlockSpec((tm, f), lambda j, i: (i, 0)),
                  pl.BlockSpec((None, f, tn), lambda j, i: (layer, 0, j), pipeline_mode=pl.Buffered(1)),
                  pl.BlockSpec((tm, tn), lambda j, i: (i, j))],
        out_specs=pl.BlockSpec((tm, tn), lambda j, i: (i, j)),
        out_shape=jax.ShapeDtypeStruct((m, d), F32),
        scratch_shapes=[pltpu.VMEM((f, tn), BF16)],
        compiler_params=_params("arbitrary", "arbitrary"),
        name="ffn_down",
    )(act, w, x2)


def _rope_tables(s_len):
    def cos_sin(dim):
        inv = 1.0 / (ROPE_THETA ** (jnp.arange(0, dim, 2, dtype=F32) / dim))
        ang = jnp.arange(s_len, dtype=F32)[:, None] * inv[None, :]
        return jnp.cos(ang), jnp.sin(ang)

    cf, sf = cos_sin(HEAD_DIM)
    ch, sh = cos_sin(DIFF_QK_DIM)
    zh = jnp.zeros_like(sh)
    return (jnp.concatenate([cf, cf], axis=-1),
            jnp.concatenate([-sf, sf], axis=-1),
            jnp.concatenate([ch, ch, ch, ch], axis=-1),
            jnp.concatenate([-sh, zh, -sh, zh], axis=-1),
            jnp.concatenate([zh, sh, zh, sh], axis=-1))


def kernel(x, w_in, b_fgate, w_out, diff_lq1, diff_lk1, diff_lq2, diff_lk2, diff_subln, attn_norm,
           w_gate, w_up, w_down, ffn_norm, final_norm):
    b, s_len, d = x.shape
    depth = w_in.shape[0]
    m = b * s_len
    tables = _rope_tables(s_len)
    b_fg = jnp.pad(b_fgate, ((0, 0), (0, LANES - HEADS_PER_GROUP)))
    w_out_bf16 = w_out.astype(BF16)

    x2 = x.reshape(m, d)
    for l in range(depth):
        h, cq, ck = _norm_gate(x2.reshape(b, s_len, d), attn_norm[l][None, :], w_in, l, b_fg[l][None, :])
        h2 = h.reshape(m, d)
        zqk, zvt = _inproj(h2, w_in, l, tables, b, s_len)
        lam_init = 0.8 - 0.6 * math.exp(-0.3 * l)
        fox = _fox_attention(zqk, zvt, cq, ck)
        moba = _moba_attention(zqk, zvt)
        diff = _diff_attention(zqk, zvt, diff_lq1[l][None, :], diff_lk1[l][None, :], diff_lq2[l][None, :],
                               diff_lk2[l][None, :], diff_subln[l][None, :], lam_init)
        sb = _sb_attention(zqk, zvt)
        x2, hf = _outproj_norm((fox, moba, diff, sb), w_out_bf16, l, x2, ffn_norm[l][None, :], s_len)
        act = _ffn_up(hf, w_gate, w_up, l)
        x2 = _ffn_down(act, w_down, l, x2)
    return _norm(x2, final_norm[None, :], F32).reshape(b, s_len, d)
```

```python
import functools
import math

import jax
import jax.numpy as jnp
from jax import lax
from jax.experimental import pallas as pl
from jax.experimental.pallas import tpu as pltpu

F32 = jnp.float32
BF16 = jnp.bfloat16

HEAD_DIM = 128
HEADS_PER_GROUP = 4
GROUP_WIDTH = HEADS_PER_GROUP * HEAD_DIM
N_MIXERS = 4
GATE_TILE = 3
DIFF_QK_DIM = HEAD_DIM // 2
MOBA_BLOCK = 256
MOBA_TOPK = 3
ROPE_THETA = 10000.0
NORM_EPS = 1e-6
DIFF_SUBLN_EPS = 1e-5
LANES = 128
SUBLANES = 8
BF16_ROWS = 16
NEG_INF = float("-inf")
LOG2E = math.log2(math.e)
ATTN_TILE = 256
HEADS_PER_STEP = 2
NORM_ROWS = 512
INPROJ_ROWS = 1024
FFN_UP_ROWS = 2048
MM_CHUNK = 256
FFN_UP_CHUNK = 512
DOWN_ROWS = 512

VMEM_LIMIT_BYTES = 56 * 1024 * 1024

MIXER_FOX, MIXER_MOBA, MIXER_DIFF, MIXER_SB = range(N_MIXERS)


def _nt_dot(a, b):
    return lax.dot_general(a, b, (((1,), (1,)), ((), ())), preferred_element_type=F32)


def _dot(a, b):
    return jnp.dot(a, b, preferred_element_type=F32)


def _split2_bf16(x):
    hi = x.astype(BF16)
    lo = (x - hi.astype(F32)).astype(BF16)
    return hi, lo


def _split3_bf16(x):
    hi = x.astype(BF16)
    r = x - hi.astype(F32)
    mid = r.astype(BF16)
    lo = (r - mid.astype(F32)).astype(BF16)
    return hi, mid, lo


def _rms_normalize(x, g, eps):
    ms = jnp.mean(x * x, axis=-1, keepdims=True)
    return x * lax.rsqrt(ms + eps) * g


def _params(*semantics):
    return pltpu.CompilerParams(dimension_semantics=semantics, vmem_limit_bytes=VMEM_LIMIT_BYTES)


def _norm_kernel(x_ref, g_ref, o_ref):
    o_ref[...] = _rms_normalize(x_ref[...], g_ref[...], NORM_EPS).astype(o_ref.dtype)


def _norm(x2, g, out_dtype):
    m, d = x2.shape
    tm = min(NORM_ROWS, m)
    return pl.pallas_call(
        _norm_kernel,
        grid=(m // tm,),
        in_specs=[pl.BlockSpec((tm, d), lambda i: (i, 0)), pl.BlockSpec((1, d), lambda i: (0, 0))],
        out_specs=pl.BlockSpec((tm, d), lambda i: (i, 0)),
        out_shape=jax.ShapeDtypeStruct((m, d), out_dtype),
        compiler_params=_params("arbitrary"),
        name="norm",
    )(x2, g)


def _norm_gate_kernel(x_ref, g_ref, wfg_ref, b_ref, h_ref, cq_ref, ck_ref, carry_ref):
    tm = x_ref.shape[1]

    @pl.when(pl.program_id(1) == 0)
    def _():
        carry_ref[...] = jnp.zeros_like(carry_ref)

    h = _rms_normalize(x_ref[0], g_ref[...], NORM_EPS).astype(BF16)
    h_ref[0] = h
    f = _dot(h, wfg_ref[...].astype(BF16)) + b_ref[...]
    ls = jnp.minimum(f, 0.0) - jnp.log1p(jnp.exp(-jnp.abs(f)))
    row = lax.broadcasted_iota(jnp.int32, (tm, tm), 0)
    col = lax.broadcasted_iota(jnp.int32, (tm, tm), 1)
    tri = jnp.where(row >= col, 1.0, 0.0).astype(BF16)
    hi, mid, lo = _split3_bf16(ls)
    c = _dot(tri, hi) + _dot(tri, mid) + _dot(tri, lo) + carry_ref[...]
    carry_ref[...] = c[tm - 1:tm, :]

    lane = lax.broadcasted_iota(jnp.int32, (tm, LANES), 1)
    for head in range(HEADS_PER_GROUP):
        parts = _split3_bf16(jnp.broadcast_to(c[:, head:head + 1] * LOG2E, (tm, LANES)))
        n_parts = len(parts)
        q_cols = jnp.where(lane < 2 * n_parts, 1.0, 0.0)
        k_cols = jnp.where(lane < n_parts, 1.0, 0.0)
        for n, part in enumerate(parts):
            q_cols = jnp.where(lane == n, part.astype(F32), q_cols)
            k_cols = jnp.where(lane == n_parts + n, -part.astype(F32), k_cols)
        cq_ref[0, head] = q_cols.astype(BF16)
        ck_ref[0, head] = k_cols.astype(BF16)


def _norm_gate(x3, g, w_in, layer, b_fg):
    b, s_len, d = x3.shape
    tm = min(NORM_ROWS, s_len)
    gate_block = GATE_TILE * GROUP_WIDTH // LANES
    aug_spec = pl.BlockSpec((1, HEADS_PER_GROUP, tm, LANES), lambda bi, i: (bi, 0, i, 0))
    aug_shape = jax.ShapeDtypeStruct((b, HEADS_PER_GROUP, s_len, LANES), BF16)
    return pl.pallas_call(
        _norm_gate_kernel,
        grid=(b, s_len // tm),
        in_specs=[pl.BlockSpec((1, tm, d), lambda bi, i: (bi, i, 0)),
                  pl.BlockSpec((1, d), lambda bi, i: (0, 0)),
                  pl.BlockSpec((None, d, LANES), lambda bi, i: (layer, 0, gate_block)),
                  pl.BlockSpec((1, LANES), lambda bi, i: (0, 0))],
        out_specs=[pl.BlockSpec((1, tm, d), lambda bi, i: (bi, i, 0)), aug_spec, aug_spec],
        out_shape=[jax.ShapeDtypeStruct((b, s_len, d), BF16), aug_shape, aug_shape],
        scratch_shapes=[pltpu.VMEM((1, LANES), F32)],
        compiler_params=_params("arbitrary", "arbitrary"),
        name="norm_gate",
    )(x3, g, w_in, b_fg)


def _inproj_kernel(h_ref, wa_ref, wb_ref, cf_ref, sf_ref, ch_ref, sha_ref, shb_ref, zqk_ref, zvt_ref, w_scr,
                   *, npos):
    mixer = pl.program_id(0)
    i = pl.program_id(1)
    tm = h_ref.shape[0]
    n_slabs = wa_ref.shape[1] // LANES
    shift = HEADS_PER_GROUP

    @pl.when((i == 0) & (mixer == MIXER_FOX))
    def _():
        w_scr[...] = wa_ref[...].astype(BF16)

    @pl.when((i == 0) & (mixer != MIXER_FOX))
    def _():
        lane = lax.broadcasted_iota(jnp.int32, (wa_ref.shape[0], LANES), 1)
        for s in range(n_slabs):
            cur = wa_ref[:, s * LANES:(s + 1) * LANES]
            nxt = wa_ref[:, (s + 1) * LANES:(s + 2) * LANES] if s + 1 < n_slabs else wb_ref[...]
            moved = jnp.where(lane < LANES - shift, pltpu.roll(cur, LANES - shift, 1),
                              pltpu.roll(nxt, LANES - shift, 1))
            w_scr[:, s * LANES:(s + 1) * LANES] = moved.astype(BF16)

    scale_full = HEAD_DIM ** -0.5 * LOG2E
    scale_half = DIFF_QK_DIM ** -0.5 * LOG2E

    def rope_full(a, pos):
        return a * cf_ref[pos, :] + pltpu.roll(a, HEAD_DIM // 2, 1) * sf_ref[pos, :]

    def rope_half(a, pos):
        return (a * ch_ref[pos, :] + pltpu.roll(a, HEAD_DIM - DIFF_QK_DIM // 2, 1) * sha_ref[pos, :]
                + pltpu.roll(a, DIFF_QK_DIM // 2, 1) * shb_ref[pos, :])

    def run(rope, q_scale):
        for r in range(tm // MM_CHUNK):
            rows = slice(r * MM_CHUNK, (r + 1) * MM_CHUNK)
            acc = _dot(h_ref[rows, :], w_scr[...])
            pos = pl.ds(pl.multiple_of((i % npos) * tm + r * MM_CHUNK, MM_CHUNK), MM_CHUNK)
            for s in range(HEADS_PER_GROUP):
                q = acc[:, s * HEAD_DIM:(s + 1) * HEAD_DIM]
                k = acc[:, GROUP_WIDTH + s * HEAD_DIM:GROUP_WIDTH + (s + 1) * HEAD_DIM]
                v = acc[:, 2 * GROUP_WIDTH + s * HEAD_DIM:2 * GROUP_WIDTH + (s + 1) * HEAD_DIM]
                if rope is not None:
                    q, k = rope(q, pos), rope(k, pos)
                zqk_ref[0, s, rows, :] = (q * q_scale).astype(BF16)
                zqk_ref[0, HEADS_PER_GROUP + s, rows, :] = k.astype(BF16)
                zvt_ref[0, s, :, rows] = jnp.transpose(v).astype(BF16)

    pl.when((mixer == MIXER_FOX) | (mixer == MIXER_SB))(functools.partial(run, None, scale_full))
    pl.when(mixer == MIXER_MOBA)(functools.partial(run, rope_full, scale_full))
    pl.when(mixer == MIXER_DIFF)(functools.partial(run, rope_half, scale_half))


def _inproj(h2, w_in, layer, tables, b, s_len):
    m, d = h2.shape
    tm = min(INPROJ_ROWS, s_len)
    assert tm % MM_CHUNK == 0
    npos = s_len // tm
    width = 3 * GROUP_WIDTH
    once = dict(pipeline_mode=pl.Buffered(1))
    tab_spec = pl.BlockSpec((s_len, LANES), lambda j, i: (0, 0), **once)
    return pl.pallas_call(
        functools.partial(_inproj_kernel, npos=npos),
        grid=(N_MIXERS, m // tm),
        in_specs=[
            pl.BlockSpec((tm, d), lambda j, i: (i, 0)),
            pl.BlockSpec((None, d, width), lambda j, i: (layer, 0, j)),
            pl.BlockSpec((None, d, LANES), lambda j, i: (layer, 0, (j + 1) * (width // LANES))),
            tab_spec, tab_spec, tab_spec, tab_spec, tab_spec,
        ],
        out_specs=[
            pl.BlockSpec((1, 2 * HEADS_PER_GROUP, tm, HEAD_DIM), lambda j, i: (i // npos, j, i % npos, 0)),
            pl.BlockSpec((1, HEADS_PER_GROUP, HEAD_DIM, tm), lambda j, i: (i // npos, j, 0, i % npos)),
        ],
        out_shape=[
            jax.ShapeDtypeStruct((b, 2 * N_MIXERS * HEADS_PER_GROUP, s_len, HEAD_DIM), BF16),
            jax.ShapeDtypeStruct((b, N_MIXERS * HEADS_PER_GROUP, HEAD_DIM, s_len), BF16),
        ],
        scratch_shapes=[pltpu.VMEM((d, width), BF16)],
        compiler_params=_params("arbitrary", "arbitrary"),
        name="inproj",
    )(h2, w_in, w_in, *tables)


def _key_query_mask(t, strict=False):
    key = lax.broadcasted_iota(jnp.int32, (t, t), 0)
    query = lax.broadcasted_iota(jnp.int32, (t, t), 1)
    return (key < query) if strict else (key <= query)


def _softmax_weights(logits):
    m = None
    for s in logits:
        mj = jnp.max(s, axis=0, keepdims=True)
        m = mj if m is None else jnp.maximum(m, mj)
    ps = [jnp.exp2(s - m) for s in logits]
    l = None
    for p in ps:
        lj = jnp.sum(p, axis=0, keepdims=True)
        l = lj if l is None else l + lj
    return ps, l


def _weighted_values(ps, vts):
    acc = None
    for p, vt in zip(ps, vts):
        aj = _dot(vt, p.astype(BF16))
        acc = aj if acc is None else acc + aj
    return acc


def _pipelined_tiles(n_tiles, head, tail, depth=1):
    states = {i: head(i) for i in range(min(depth, n_tiles))}
    for i in range(n_tiles):
        if i + depth < n_tiles:
            states[i + depth] = head(i + depth)
        tail(i, states.pop(i))


def _fox_kernel(hh, q_ref, k_ref, vt_ref, cq_ref, ck_ref, o_ref, *, t):
    s_len = q_ref.shape[2]
    diag = _key_query_mask(t)

    def keys(a, b):
        return jnp.concatenate([k_ref[0, hh, a:b, :], ck_ref[0, hh, a:b, :]], axis=1)

    def head(i):
        lo, hi = i * t, (i + 1) * t
        q = jnp.concatenate([q_ref[0, hh, lo:hi, :], cq_ref[0, hh, lo:hi, :]], axis=1)
        logits, vts = [], []
        if i > 0:
            logits.append(_nt_dot(keys(0, lo), q))
            vts.append(vt_ref[0, hh, :, :lo])
        logits.append(jnp.where(diag, _nt_dot(keys(lo, hi), q), NEG_INF))
        vts.append(vt_ref[0, hh, :, lo:hi])
        return logits, vts

    def tail(i, state):
        logits, vts = state
        ps, l = _softmax_weights(logits)
        out_t = _weighted_values(ps, vts) * (1.0 / l)
        o_ref[0, hh, i * t:(i + 1) * t, :] = jnp.transpose(out_t).astype(o_ref.dtype)

    _pipelined_tiles(s_len // t, head, tail, depth=2)


def _moba_kernel(hh, q_ref, k_ref, vt_ref, o_ref, *, t):
    s_len = q_ref.shape[2]

    brow = lax.broadcasted_iota(jnp.int32, (BF16_ROWS, s_len), 0)
    bcol = lax.broadcasted_iota(jnp.int32, (BF16_ROWS, s_len), 1)
    member = jnp.where(bcol // t == brow, 1.0, 0.0).astype(BF16)
    k_mean = _dot(member, k_ref[0, hh]) * (1.0 / t)
    km_parts = _split3_bf16(k_mean)
    block = lax.broadcasted_iota(jnp.int32, (BF16_ROWS, t), 0)
    diag = _key_query_mask(t)

    def head(i):
        lo, hi = i * t, (i + 1) * t
        q = q_ref[0, hh, lo:hi, :]
        s_past = _nt_dot(k_ref[0, hh, :lo, :], q) if i > 0 else None
        s_diag = _nt_dot(k_ref[0, hh, lo:hi, :], q)
        gate = sum(_nt_dot(part, q) for part in km_parts) if i > MOBA_TOPK else None
        return s_past, s_diag, gate

    def tail(i, state):
        lo, hi = i * t, (i + 1) * t
        s_past, s_diag, gate = state
        logits, vts = [], []
        if 0 < i <= MOBA_TOPK:
            logits.append(s_past)
            vts.append(vt_ref[0, hh, :, :lo])
        elif i > MOBA_TOPK:
            gate = jnp.where(block < i, gate, NEG_INF)
            beaten = jnp.zeros((BF16_ROWS, t), F32)
            for mblk in range(i):
                gm = gate[mblk:mblk + 1, :]
                wins = jnp.where(gm > gate, 1.0, jnp.where((gm == gate) & (mblk < block), 1.0, 0.0))
                beaten = beaten + wins
            for n in range(i):
                keep_n = beaten[n:n + 1, :] < MOBA_TOPK
                logits.append(jnp.where(keep_n, s_past[n * t:(n + 1) * t, :], NEG_INF))
                vts.append(vt_ref[0, hh, :, n * t:(n + 1) * t])
        logits.append(jnp.where(diag, s_diag, NEG_INF))
        vts.append(vt_ref[0, hh, :, lo:hi])
        ps, l = _softmax_weights(logits)
        out_t = _weighted_values(ps, vts) * (1.0 / l)
        o_ref[0, hh, lo:hi, :] = jnp.transpose(out_t).astype(o_ref.dtype)

    _pipelined_tiles(s_len // t, head, tail, depth=2)


def _diff_kernel(hh, lam_init_ref, q_ref, k_ref, vt_ref, lq1_ref, lk1_ref, lq2_ref, lk2_ref, gsub_ref, o_ref,
                 *, t):
    s_len = q_ref.shape[2]
    lam_init = lam_init_ref[0]
    lam = (jnp.exp(jnp.sum(lq1_ref[...] * lk1_ref[...], axis=-1, keepdims=True))
           - jnp.exp(jnp.sum(lq2_ref[...] * lk2_ref[...], axis=-1, keepdims=True)) + lam_init)
    lane = lax.broadcasted_iota(jnp.int32, (t, LANES), 1)
    diag = _key_query_mask(t)

    def head(i):
        lo, hi = i * t, (i + 1) * t
        q = q_ref[0, hh, lo:hi, :]
        zero = jnp.zeros_like(q)
        halves = (jnp.where(lane < DIFF_QK_DIM, q, zero), jnp.where(lane >= DIFF_QK_DIM, q, zero))
        both = []
        for qh in halves:
            logits = []
            if i > 0:
                logits.append(_nt_dot(k_ref[0, hh, :lo, :], qh))
            logits.append(jnp.where(diag, _nt_dot(k_ref[0, hh, lo:hi, :], qh), NEG_INF))
            both.append(logits)
        return both

    def tail(i, both):
        lo, hi = i * t, (i + 1) * t
        (ps1, l1), (ps2, l2) = (_softmax_weights(logits) for logits in both)
        ratio = lam * l1 / l2
        ps = [p1 - p2 * ratio for p1, p2 in zip(ps1, ps2)]
        vts = ([vt_ref[0, hh, :, :lo]] if i > 0 else []) + [vt_ref[0, hh, :, lo:hi]]
        out = jnp.transpose(_weighted_values(ps, vts) * (1.0 / l1))
        y = _rms_normalize(out, gsub_ref[...], DIFF_SUBLN_EPS)
        o_ref[0, hh, lo:hi, :] = (y * (1.0 - lam_init)).astype(o_ref.dtype)

    _pipelined_tiles(s_len // t, head, tail)


def _sb_kernel(hh, q_ref, k_ref, vt_ref, o_ref, *, t):
    s_len = q_ref.shape[2]
    row = lax.broadcasted_iota(jnp.int32, (t, t), 0)
    col = lax.broadcasted_iota(jnp.int32, (t, t), 1)
    later = jnp.where(col > row, 1.0, 0.0).astype(BF16)
    strict = _key_query_mask(t, strict=True)

    def head(i):
        hi = (i + 1) * t
        z_all = _nt_dot(k_ref[0, hh, :hi, :], q_ref[0, hh, i * t:hi, :])
        blocks = []
        for n in range(i, -1, -1):
            z = z_all[n * t:(n + 1) * t, :]
            sp = jnp.log2(1.0 + jnp.exp2(-jnp.abs(z)))
            log_beta = jnp.minimum(z, 0.0) - sp
            log_1m = log_beta - z
            if n == i:
                log_1m = jnp.where(strict, log_1m, 0.0)
            l_hi, l_lo = _split2_bf16(log_1m)
            within = _dot(later, l_hi) + _dot(later, l_lo)
            blocks.append((n, log_beta, within, within[0:1, :] + log_1m[0:1, :]))
        return blocks

    def tail(i, blocks):
        run = None
        acc = None
        for n, log_beta, within, total in blocks:
            a = jnp.exp2(log_beta + within)
            if n == i:
                a = jnp.where(strict, a, 0.0)
            local = _dot(vt_ref[0, hh, :, n * t:(n + 1) * t], a.astype(BF16))
            acc = local if acc is None else acc + local * jnp.exp2(run)
            run = total if run is None else run + total
        o_ref[0, hh, i * t:(i + 1) * t, :] = jnp.transpose(acc).astype(o_ref.dtype)

    _pipelined_tiles(s_len // t, head, tail)


def _attention_call(kernel_fn, name, zqk, zvt, mixer, extra_inputs=(), extra_specs=(), prefix_inputs=(),
                    prefix_specs=()):
    b, _, s_len, _ = zqk.shape
    t = min(ATTN_TILE, s_len)
    hq = 2 * mixer * HEADS_PER_GROUP
    hk = hq + HEADS_PER_GROUP
    hv = mixer * HEADS_PER_GROUP
    hs = HEADS_PER_STEP
    seq_spec = lambda h0: pl.BlockSpec((1, hs, s_len, HEAD_DIM), lambda bi, h: (bi, h0 // hs + h, 0, 0))

    def body(*refs):
        for hh in range(hs):
            kernel_fn(hh, *refs, t=t)

    return pl.pallas_call(
        body,
        grid=(b, HEADS_PER_GROUP // hs),
        in_specs=[*prefix_specs, seq_spec(hq), seq_spec(hk),
                  pl.BlockSpec((1, hs, HEAD_DIM, s_len), lambda bi, h: (bi, hv // hs + h, 0, 0)), *extra_specs],
        out_specs=seq_spec(0),
        out_shape=jax.ShapeDtypeStruct((b, HEADS_PER_GROUP, s_len, HEAD_DIM), BF16),
        compiler_params=_params("arbitrary", "arbitrary"),
        name=name,
    )(*prefix_inputs, zqk, zqk, zvt, *extra_inputs)


def _fox_attention(zqk, zvt, cq, ck):
    s_len = zqk.shape[2]
    aug_spec = pl.BlockSpec((1, HEADS_PER_STEP, s_len, LANES), lambda bi, h: (bi, h, 0, 0))
    return _attention_call(_fox_kernel, "fox", zqk, zvt, MIXER_FOX, (cq, ck), (aug_spec, aug_spec))


def _moba_attention(zqk, zvt):
    s_len = zqk.shape[2]
    assert ATTN_TILE == MOBA_BLOCK and s_len % MOBA_BLOCK == 0 and s_len // MOBA_BLOCK <= BF16_ROWS
    return _attention_call(_moba_kernel, "moba", zqk, zvt, MIXER_MOBA)


def _diff_attention(zqk, zvt, lq1, lk1, lq2, lk2, gsub, lam_init):
    vec = lambda n: pl.BlockSpec((1, n), lambda bi, h: (0, 0))
    return _attention_call(
        _diff_kernel, "diff", zqk, zvt, MIXER_DIFF, (lq1, lk1, lq2, lk2, gsub),
        (vec(DIFF_QK_DIM), vec(DIFF_QK_DIM), vec(DIFF_QK_DIM), vec(DIFF_QK_DIM), vec(HEAD_DIM)),
        prefix_inputs=(jnp.full((1,), lam_init, F32),),
        prefix_specs=(pl.BlockSpec(memory_space=pltpu.SMEM),))


def _sb_attention(zqk, zvt):
    return _attention_call(_sb_kernel, "stickbreak", zqk, zvt, MIXER_SB)


def _cast_weights_once(w_ref, w_scr):
    @pl.when(pl.program_id(1) == 0)
    def _():
        w_scr[...] = w_ref[...].astype(BF16)


def _outproj_norm_kernel(a0_ref, a1_ref, a2_ref, a3_ref, w_ref, x_ref, g_ref, o_ref, h_ref):
    tm = x_ref.shape[0]
    for r in range(tm // MM_CHUNK):
        rows = slice(r * MM_CHUNK, (r + 1) * MM_CHUNK)
        heads = [a_ref[0, s, rows, :] for a_ref in (a0_ref, a1_ref, a2_ref, a3_ref)
                 for s in range(HEADS_PER_GROUP)]
        mixed = jnp.concatenate(heads, axis=-1)
        y = x_ref[rows, :] + _dot(mixed, w_ref[...])
        o_ref[rows, :] = y
        h_ref[rows, :] = _rms_normalize(y, g_ref[...], NORM_EPS).astype(BF16)


def _outproj_norm(mixed, w_bf16, layer, x2, g, s_len):
    m, d = x2.shape
    tm = min(NORM_ROWS, s_len)
    assert tm % MM_CHUNK == 0
    npos = s_len // tm
    a_spec = pl.BlockSpec((1, HEADS_PER_GROUP, tm, HEAD_DIM), lambda i: (i // npos, 0, i % npos, 0))
    row_spec = pl.BlockSpec((tm, d), lambda i: (i, 0))
    return pl.pallas_call(
        _outproj_norm_kernel,
        grid=(m // tm,),
        in_specs=[a_spec, a_spec, a_spec, a_spec,
                  pl.BlockSpec((None, d, d), lambda i: (layer, 0, 0)),
                  row_spec,
                  pl.BlockSpec((1, d), lambda i: (0, 0))],
        out_specs=[row_spec, row_spec],
        out_shape=[jax.ShapeDtypeStruct((m, d), F32), jax.ShapeDtypeStruct((m, d), BF16)],
        compiler_params=_params("arbitrary"),
        name="outproj_norm",
    )(*mixed, w_bf16, x2, g)


def _ffn_up_kernel(h_ref, wg_ref, wu_ref, o_ref, wg_scr, wu_scr):
    _cast_weights_once(wg_ref, wg_scr)
    _cast_weights_once(wu_ref, wu_scr)
    chunk = min(FFN_UP_CHUNK, h_ref.shape[0])
    for r in range(h_ref.shape[0] // chunk):
        rows = slice(r * chunk, (r + 1) * chunk)
        h = h_ref[rows, :]
        gate = _dot(h, wg_scr[...])
        up = _dot(h, wu_scr[...])
        o_ref[rows, :] = (gate / (1.0 + jnp.exp(-gate)) * up).astype(BF16)


def _ffn_up(h2, wg, wu, layer, tn=512):
    m, d = h2.shape
    f = wg.shape[2]
    tm = min(FFN_UP_ROWS, m)
    w_spec = pl.BlockSpec((None, d, tn), lambda j, i: (layer, 0, j))
    return pl.pallas_call(
        _ffn_up_kernel,
        grid=(f // tn, m // tm),
        in_specs=[pl.BlockSpec((tm, d), lambda j, i: (i, 0)), w_spec, w_spec],
        out_specs=pl.BlockSpec((tm, tn), lambda j, i: (i, j)),
        out_shape=jax.ShapeDtypeStruct((m, f), BF16),
        scratch_shapes=[pltpu.VMEM((d, tn), BF16), pltpu.VMEM((d, tn), BF16)],
        compiler_params=_params("arbitrary", "arbitrary"),
        name="ffn_up",
    )(h2, wg, wu)


def _ffn_down_kernel(a_ref, w_ref, x_ref, o_ref, w_scr):
    _cast_weights_once(w_ref, w_scr)
    o_ref[...] = x_ref[...] + _dot(a_ref[...], w_scr[...])


def _ffn_down(act, w, layer, x2, tn=1024):
    m, d = x2.shape
    f = act.shape[1]
    tm = min(DOWN_ROWS, m)
    return pl.pallas_call(
        _ffn_down_kernel,
        grid=(d // tn, m // tm),
        in_specs=[pl.BlockSpec((tm, f), lambda j, i: (i, 0)),
                  pl.BlockSpec((None, f, tn), lambda j, i: (layer, 0, j), pipeline_mode=pl.Buffered(1)),
                  pl.BlockSpec((tm, tn), lambda j, i: (i, j))],
        out_specs=pl.BlockSpec((tm, tn), lambda j, i: (i, j)),
        out_shape=jax.ShapeDtypeStruct((m, d), F32),
        scratch_shapes=[pltpu.VMEM((f, tn), BF16)],
        compiler_params=_params("arbitrary", "arbitrary"),
        name="ffn_down",
    )(act, w, x2)


def _rope_tables(s_len):
    def cos_sin(dim):
        inv = 1.0 / (ROPE_THETA ** (jnp.arange(0, dim, 2, dtype=F32) / dim))
        ang = jnp.arange(s_len, dtype=F32)[:, None] * inv[None, :]
        return jnp.cos(ang), jnp.sin(ang)

    cf, sf = cos_sin(HEAD_DIM)
    ch, sh = cos_sin(DIFF_QK_DIM)
    zh = jnp.zeros_like(sh)
    return (jnp.concatenate([cf, cf], axis=-1),
            jnp.concatenate([-sf, sf], axis=-1),
            jnp.concatenate([ch, ch, ch, ch], axis=-1),
            jnp.concatenate([-sh, zh, -sh, zh], axis=-1),
            jnp.concatenate([zh, sh, zh, sh], axis=-1))


def kernel(x, w_in, b_fgate, w_out, diff_lq1, diff_lk1, diff_lq2, diff_lk2, diff_subln, attn_norm,
           w_gate, w_up, w_down, ffn_norm, final_norm):
    b, s_len, d = x.shape
    depth = w_in.shape[0]
    m = b * s_len
    tables = _rope_tables(s_len)
    b_fg = jnp.pad(b_fgate, ((0, 0), (0, LANES - HEADS_PER_GROUP)))
    w_out_bf16 = w_out.astype(BF16)

    x2 = x.reshape(m, d)
    for l in range(depth):
        h, cq, ck = _norm_gate(x2.reshape(b, s_len, d), attn_norm[l][None, :], w_in, l, b_fg[l][None, :])
        h2 = h.reshape(m, d)
        zqk, zvt = _inproj(h2, w_in, l, tables, b, s_len)
        lam_init = 0.8 - 0.6 * math.exp(-0.3 * l)
        fox = _fox_attention(zqk, zvt, cq, ck)
        moba = _moba_attention(zqk, zvt)
        diff = _diff_attention(zqk, zvt, diff_lq1[l][None, :], diff_lk1[l][None, :], diff_lq2[l][None, :],
                               diff_lk2[l][None, :], diff_subln[l][None, :], lam_init)
        sb = _sb_attention(zqk, zvt)
        x2, hf = _outproj_norm((fox, moba, diff, sb), w_out_bf16, l, x2, ffn_norm[l][None, :], s_len)
        act = _ffn_up(hf, w_gate, w_up, l)
        x2 = _ffn_down(act, w_down, l, x2)
    return _norm(x2, final_norm[None, :], F32).reshape(b, s_len, d)
```

```python
import functools
import math

import jax
import jax.numpy as jnp
from jax import lax
from jax.experimental import pallas as pl
from jax.experimental.pallas import tpu as pltpu

F32 = jnp.float32
BF16 = jnp.bfloat16

HEAD_DIM = 128
HEADS_PER_GROUP = 4
GROUP_WIDTH = HEADS_PER_GROUP * HEAD_DIM
N_MIXERS = 4
GATE_TILE = 3
DIFF_QK_DIM = HEAD_DIM // 2
MOBA_BLOCK = 256
MOBA_TOPK = 3
ROPE_THETA = 10000.0
NORM_EPS = 1e-6
DIFF_SUBLN_EPS = 1e-5
LANES = 128
SUBLANES = 8
BF16_ROWS = 16
NEG_INF = float("-inf")
LOG2E = math.log2(math.e)
ATTN_TILE = 256
HEADS_PER_STEP = 2
NORM_ROWS = 512
INPROJ_ROWS = 1024
FFN_UP_ROWS = 2048
MM_CHUNK = 256
FFN_UP_CHUNK = 512
DOWN_ROWS = 512

VMEM_LIMIT_BYTES = 56 * 1024 * 1024

MIXER_FOX, MIXER_MOBA, MIXER_DIFF, MIXER_SB = range(N_MIXERS)


def _nt_dot(a, b):
    return lax.dot_general(a, b, (((1,), (1,)), ((), ())), preferred_element_type=F32)


def _dot(a, b):
    return jnp.dot(a, b, preferred_element_type=F32)


def _split2_bf16(x):
    hi = x.astype(BF16)
    lo = (x - hi.astype(F32)).astype(BF16)
    return hi, lo


def _split3_bf16(x):
    hi = x.astype(BF16)
    r = x - hi.astype(F32)
    mid = r.astype(BF16)
    lo = (r - mid.astype(F32)).astype(BF16)
    return hi, mid, lo


def _rms_normalize(x, g, eps):
    ms = jnp.mean(x * x, axis=-1, keepdims=True)
    return x * lax.rsqrt(ms + eps) * g


def _params(*semantics):
    return pltpu.CompilerParams(dimension_semantics=semantics, vmem_limit_bytes=VMEM_LIMIT_BYTES)


def _norm_kernel(x_ref, g_ref, o_ref):
    o_ref[...] = _rms_normalize(x_ref[...], g_ref[...], NORM_EPS).astype(o_ref.dtype)


def _norm(x2, g, out_dtype):
    m, d = x2.shape
    tm = min(NORM_ROWS, m)
    return pl.pallas_call(
        _norm_kernel,
        grid=(m // tm,),
        in_specs=[pl.BlockSpec((tm, d), lambda i: (i, 0)), pl.BlockSpec((1, d), lambda i: (0, 0))],
        out_specs=pl.BlockSpec((tm, d), lambda i: (i, 0)),
        out_shape=jax.ShapeDtypeStruct((m, d), out_dtype),
        compiler_params=_params("arbitrary"),
        name="norm",
    )(x2, g)


def _norm_gate_kernel(x_ref, g_ref, wfg_ref, b_ref, h_ref, cq_ref, ck_ref, carry_ref):
    tm = x_ref.shape[1]

    @pl.when(pl.program_id(1) == 0)
    def _():
        carry_ref[...] = jnp.zeros_like(carry_ref)

    h = _rms_normalize(x_ref[0], g_ref[...], NORM_EPS).astype(BF16)
    h_ref[0] = h
    f = _dot(h, wfg_ref[...].astype(BF16)) + b_ref[...]
    ls = jnp.minimum(f, 0.0) - jnp.log1p(jnp.exp(-jnp.abs(f)))
    row = lax.broadcasted_iota(jnp.int32, (tm, tm), 0)
    col = lax.broadcasted_iota(jnp.int32, (tm, tm), 1)
    tri = jnp.where(row >= col, 1.0, 0.0).astype(BF16)
    hi, mid, lo = _split3_bf16(ls)
    c = _dot(tri, hi) + _dot(tri, mid) + _dot(tri, lo) + carry_ref[...]
    carry_ref[...] = c[tm - 1:tm, :]

    lane = lax.broadcasted_iota(jnp.int32, (tm, LANES), 1)
    for head in range(HEADS_PER_GROUP):
        parts = _split3_bf16(jnp.broadcast_to(c[:, head:head + 1] * LOG2E, (tm, LANES)))
        n_parts = len(parts)
        q_cols = jnp.where(lane < 2 * n_parts, 1.0, 0.0)
        k_cols = jnp.where(lane < n_parts, 1.0, 0.0)
        for n, part in enumerate(parts):
            q_cols = jnp.where(lane == n, part.astype(F32), q_cols)
            k_cols = jnp.where(lane == n_parts + n, -part.astype(F32), k_cols)
        cq_ref[0, head] = q_cols.astype(BF16)
        ck_ref[0, head] = k_cols.astype(BF16)


def _norm_gate(x3, g, w_in, layer, b_fg):
    b, s_len, d = x3.shape
    tm = min(NORM_ROWS, s_len)
    gate_block = GATE_TILE * GROUP_WIDTH // LANES
    aug_spec = pl.BlockSpec((1, HEADS_PER_GROUP, tm, LANES), lambda bi, i: (bi, 0, i, 0))
    aug_shape = jax.ShapeDtypeStruct((b, HEADS_PER_GROUP, s_len, LANES), BF16)
    return pl.pallas_call(
        _norm_gate_kernel,
        grid=(b, s_len // tm),
        in_specs=[pl.BlockSpec((1, tm, d), lambda bi, i: (bi, i, 0)),
                  pl.BlockSpec((1, d), lambda bi, i: (0, 0)),
                  pl.BlockSpec((None, d, LANES), lambda bi, i: (layer, 0, gate_block)),
                  pl.BlockSpec((1, LANES), lambda bi, i: (0, 0))],
        out_specs=[pl.BlockSpec((1, tm, d), lambda bi, i: (bi, i, 0)), aug_spec, aug_spec],
        out_shape=[jax.ShapeDtypeStruct((b, s_len, d), BF16), aug_shape, aug_shape],
        scratch_shapes=[pltpu.VMEM((1, LANES), F32)],
        compiler_params=_params("arbitrary", "arbitrary"),
        name="norm_gate",
    )(x3, g, w_in, b_fg)


def _inproj_kernel(h_ref, wa_ref, wb_ref, cf_ref, sf_ref, ch_ref, sha_ref, shb_ref, zqk_ref, zvt_ref, w_scr,
                   *, npos):
    mixer = pl.program_id(0)
    i = pl.program_id(1)
    tm = h_ref.shape[0]
    n_slabs = wa_ref.shape[1] // LANES
    shift = HEADS_PER_GROUP

    @pl.when((i == 0) & (mixer == MIXER_FOX))
    def _():
        w_scr[...] = wa_ref[...].astype(BF16)

    @pl.when((i == 0) & (mixer != MIXER_FOX))
    def _():
        lane = lax.broadcasted_iota(jnp.int32, (wa_ref.shape[0], LANES), 1)
        for s in range(n_slabs):
            cur = wa_ref[:, s * LANES:(s + 1) * LANES].astype(F32)
            nxt = (wa_ref[:, (s + 1) * LANES:(s + 2) * LANES] if s + 1 < n_slabs else wb_ref[...]).astype(F32)
            moved = jnp.where(lane < LANES - shift, pltpu.roll(cur, LANES - shift, 1),
                              pltpu.roll(nxt, LANES - shift, 1))
            w_scr[:, s * LANES:(s + 1) * LANES] = moved.astype(BF16)

    scale_full = HEAD_DIM ** -0.5 * LOG2E
    scale_half = DIFF_QK_DIM ** -0.5 * LOG2E

    def rope_full(a, pos):
        return a * cf_ref[pos, :] + pltpu.roll(a, HEAD_DIM // 2, 1) * sf_ref[pos, :]

    def rope_half(a, pos):
        return (a * ch_ref[pos, :] + pltpu.roll(a, HEAD_DIM - DIFF_QK_DIM // 2, 1) * sha_ref[pos, :]
                + pltpu.roll(a, DIFF_QK_DIM // 2, 1) * shb_ref[pos, :])

    def run(rope, q_scale):
        for r in range(tm // MM_CHUNK):
            rows = slice(r * MM_CHUNK, (r + 1) * MM_CHUNK)
            acc = _dot(h_ref[rows, :], w_scr[...])
            pos = pl.ds(pl.multiple_of((i % npos) * tm + r * MM_CHUNK, MM_CHUNK), MM_CHUNK)
            for s in range(HEADS_PER_GROUP):
                q = acc[:, s * HEAD_DIM:(s + 1) * HEAD_DIM]
                k = acc[:, GROUP_WIDTH + s * HEAD_DIM:GROUP_WIDTH + (s + 1) * HEAD_DIM]
                v = acc[:, 2 * GROUP_WIDTH + s * HEAD_DIM:2 * GROUP_WIDTH + (s + 1) * HEAD_DIM]
                if rope is not None:
                    q, k = rope(q, pos), rope(k, pos)
                zqk_ref[0, s, rows, :] = (q * q_scale).astype(BF16)
                zqk_ref[0, HEADS_PER_GROUP + s, rows, :] = k.astype(BF16)
                zvt_ref[0, s, :, rows] = jnp.transpose(v).astype(BF16)

    pl.when((mixer == MIXER_FOX) | (mixer == MIXER_SB))(functools.partial(run, None, scale_full))
    pl.when(mixer == MIXER_MOBA)(functools.partial(run, rope_full, scale_full))
    pl.when(mixer == MIXER_DIFF)(functools.partial(run, rope_half, scale_half))


def _inproj(h2, w_in, layer, tables, b, s_len):
    m, d = h2.shape
    tm = min(INPROJ_ROWS, s_len)
    assert tm % MM_CHUNK == 0
    npos = s_len // tm
    width = 3 * GROUP_WIDTH
    once = dict(pipeline_mode=pl.Buffered(1))
    tab_spec = pl.BlockSpec((s_len, LANES), lambda j, i: (0, 0), **once)
    return pl.pallas_call(
        functools.partial(_inproj_kernel, npos=npos),
        grid=(N_MIXERS, m // tm),
        in_specs=[
            pl.BlockSpec((tm, d), lambda j, i: (i, 0)),
            pl.BlockSpec((None, d, width), lambda j, i: (layer, 0, j)),
            pl.BlockSpec((None, d, LANES), lambda j, i: (layer, 0, (j + 1) * (width // LANES))),
            tab_spec, tab_spec, tab_spec, tab_spec, tab_spec,
        ],
        out_specs=[
            pl.BlockSpec((1, 2 * HEADS_PER_GROUP, tm, HEAD_DIM), lambda j, i: (i // npos, j, i % npos, 0)),
            pl.BlockSpec((1, HEADS_PER_GROUP, HEAD_DIM, tm), lambda j, i: (i // npos, j, 0, i % npos)),
        ],
        out_shape=[
            jax.ShapeDtypeStruct((b, 2 * N_MIXERS * HEADS_PER_GROUP, s_len, HEAD_DIM), BF16),
            jax.ShapeDtypeStruct((b, N_MIXERS * HEADS_PER_GROUP, HEAD_DIM, s_len), BF16),
        ],
        scratch_shapes=[pltpu.VMEM((d, width), BF16)],
        compiler_params=_params("arbitrary", "arbitrary"),
        name="inproj",
    )(h2, w_in, w_in, *tables)


def _key_query_mask(t, strict=False):
    key = lax.broadcasted_iota(jnp.int32, (t, t), 0)
    query = lax.broadcasted_iota(jnp.int32, (t, t), 1)
    return (key < query) if strict else (key <= query)


def _softmax_weights(logits):
    m = None
    for s in logits:
        mj = jnp.max(s, axis=0, keepdims=True)
        m = mj if m is None else jnp.maximum(m, mj)
    ps = [jnp.exp2(s - m) for s in logits]
    l = None
    for p in ps:
        lj = jnp.sum(p, axis=0, keepdims=True)
        l = lj if l is None else l + lj
    return ps, l


def _weighted_values(ps, vts):
    acc = None
    for p, vt in zip(ps, vts):
        aj = _dot(vt, p.astype(BF16))
        acc = aj if acc is None else acc + aj
    return acc


def _pipelined_tiles(n_tiles, head, tail, depth=1):
    states = {i: head(i) for i in range(min(depth, n_tiles))}
    for i in range(n_tiles):
        if i + depth < n_tiles:
            states[i + depth] = head(i + depth)
        tail(i, states.pop(i))


def _fox_kernel(hh, q_ref, k_ref, vt_ref, cq_ref, ck_ref, o_ref, *, t):
    s_len = q_ref.shape[2]
    diag = _key_query_mask(t)

    def keys(a, b):
        return jnp.concatenate([k_ref[0, hh, a:b, :], ck_ref[0, hh, a:b, :]], axis=1)

    def head(i):
        lo, hi = i * t, (i + 1) * t
        q = jnp.concatenate([q_ref[0, hh, lo:hi, :], cq_ref[0, hh, lo:hi, :]], axis=1)
        logits, vts = [], []
        if i > 0:
            logits.append(_nt_dot(keys(0, lo), q))
            vts.append(vt_ref[0, hh, :, :lo])
        logits.append(jnp.where(diag, _nt_dot(keys(lo, hi), q), NEG_INF))
        vts.append(vt_ref[0, hh, :, lo:hi])
        return logits, vts

    def tail(i, state):
        logits, vts = state
        ps, l = _softmax_weights(logits)
        out_t = _weighted_values(ps, vts) * (1.0 / l)
        o_ref[0, hh, i * t:(i + 1) * t, :] = jnp.transpose(out_t).astype(o_ref.dtype)

    _pipelined_tiles(s_len // t, head, tail, depth=2)


def _moba_kernel(hh, q_ref, k_ref, vt_ref, o_ref, *, t):
    s_len = q_ref.shape[2]

    brow = lax.broadcasted_iota(jnp.int32, (BF16_ROWS, s_len), 0)
    bcol = lax.broadcasted_iota(jnp.int32, (BF16_ROWS, s_len), 1)
    member = jnp.where(bcol // t == brow, 1.0, 0.0).astype(BF16)
    k_mean = _dot(member, k_ref[0, hh]) * (1.0 / t)
    km_parts = _split3_bf16(k_mean)
    block = lax.broadcasted_iota(jnp.int32, (BF16_ROWS, t), 0)
    diag = _key_query_mask(t)

    def head(i):
        lo, hi = i * t, (i + 1) * t
        q = q_ref[0, hh, lo:hi, :]
        s_past = _nt_dot(k_ref[0, hh, :lo, :], q) if i > 0 else None
        s_diag = _nt_dot(k_ref[0, hh, lo:hi, :], q)
        gate = sum(_nt_dot(part, q) for part in km_parts) if i > MOBA_TOPK else None
        return s_past, s_diag, gate

    def tail(i, state):
        lo, hi = i * t, (i + 1) * t
        s_past, s_diag, gate = state
        logits, vts = [], []
        if 0 < i <= MOBA_TOPK:
            logits.append(s_past)
            vts.append(vt_ref[0, hh, :, :lo])
        elif i > MOBA_TOPK:
            gate = jnp.where(block < i, gate, NEG_INF)
            beaten = jnp.zeros((BF16_ROWS, t), F32)
            for mblk in range(i):
                gm = gate[mblk:mblk + 1, :]
                wins = jnp.where(gm > gate, 1.0, jnp.where((gm == gate) & (mblk < block), 1.0, 0.0))
                beaten = beaten + wins
            for n in range(i):
                keep_n = beaten[n:n + 1, :] < MOBA_TOPK
                logits.append(jnp.where(keep_n, s_past[n * t:(n + 1) * t, :], NEG_INF))
                vts.append(vt_ref[0, hh, :, n * t:(n + 1) * t])
        logits.append(jnp.where(diag, s_diag, NEG_INF))
        vts.append(vt_ref[0, hh, :, lo:hi])
        ps, l = _softmax_weights(logits)
        out_t = _weighted_values(ps, vts) * (1.0 / l)
        o_ref[0, hh, lo:hi, :] = jnp.transpose(out_t).astype(o_ref.dtype)

    _pipelined_tiles(s_len // t, head, tail, depth=2)


def _diff_kernel(hh, lam_init_ref, q_ref, k_ref, vt_ref, lq1_ref, lk1_ref, lq2_ref, lk2_ref, gsub_ref, o_ref,
                 *, t):
    s_len = q_ref.shape[2]
    lam_init = lam_init_ref[0]
    lam = (jnp.exp(jnp.sum(lq1_ref[...] * lk1_ref[...], axis=-1, keepdims=True))
           - jnp.exp(jnp.sum(lq2_ref[...] * lk2_ref[...], axis=-1, keepdims=True)) + lam_init)
    lane = lax.broadcasted_iota(jnp.int32, (t, LANES), 1)
    diag = _key_query_mask(t)

    def head(i):
        lo, hi = i * t, (i + 1) * t
        q = q_ref[0, hh, lo:hi, :]
        zero = jnp.zeros_like(q)
        halves = (jnp.where(lane < DIFF_QK_DIM, q, zero), jnp.where(lane >= DIFF_QK_DIM, q, zero))
        both = []
        for qh in halves:
            logits = []
            if i > 0:
                logits.append(_nt_dot(k_ref[0, hh, :lo, :], qh))
            logits.append(jnp.where(diag, _nt_dot(k_ref[0, hh, lo:hi, :], qh), NEG_INF))
            both.append(logits)
        return both

    def tail(i, both):
        lo, hi = i * t, (i + 1) * t
        (ps1, l1), (ps2, l2) = (_softmax_weights(logits) for logits in both)
        ratio = lam * l1 / l2
        ps = [p1 - p2 * ratio for p1, p2 in zip(ps1, ps2)]
        vts = ([vt_ref[0, hh, :, :lo]] if i > 0 else []) + [vt_ref[0, hh, :, lo:hi]]
        out = jnp.transpose(_weighted_values(ps, vts) * (1.0 / l1))
        y = _rms_normalize(out, gsub_ref[...], DIFF_SUBLN_EPS)
        o_ref[0, hh, lo:hi, :] = (y * (1.0 - lam_init)).astype(o_ref.dtype)

    _pipelined_tiles(s_len // t, head, tail)


def _sb_kernel(hh, q_ref, k_ref, vt_ref, o_ref, *, t):
    s_len = q_ref.shape[2]
    row = lax.broadcasted_iota(jnp.int32, (t, t), 0)
    col = lax.broadcasted_iota(jnp.int32, (t, t), 1)
    later = jnp.where(col > row, 1.0, 0.0).astype(BF16)
    strict = _key_query_mask(t, strict=True)

    def head(i):
        hi = (i + 1) * t
        z_all = _nt_dot(k_ref[0, hh, :hi, :], q_ref[0, hh, i * t:hi, :])
        blocks = []
        for n in range(i, -1, -1):
            z = z_all[n * t:(n + 1) * t, :]
            sp = jnp.log2(1.0 + jnp.exp2(-jnp.abs(z)))
            log_beta = jnp.minimum(z, 0.0) - sp
            log_1m = log_beta - z
            if n == i:
                log_1m = jnp.where(strict, log_1m, 0.0)
            l_hi, l_lo = _split2_bf16(log_1m)
            within = _dot(later, l_hi) + _dot(later, l_lo)
            blocks.append((n, log_beta, within, within[0:1, :] + log_1m[0:1, :]))
        return blocks

    def tail(i, blocks):
        run = None
        acc = None
        for n, log_beta, within, total in blocks:
            a = jnp.exp2(log_beta + within)
            if n == i:
                a = jnp.where(strict, a, 0.0)
            local = _dot(vt_ref[0, hh, :, n * t:(n + 1) * t], a.astype(BF16))
            acc = local if acc is None else acc + local * jnp.exp2(run)
            run = total if run is None else run + total
        o_ref[0, hh, i * t:(i + 1) * t, :] = jnp.transpose(acc).astype(o_ref.dtype)

    _pipelined_tiles(s_len // t, head, tail)


def _attention_call(kernel_fn, name, zqk, zvt, mixer, extra_inputs=(), extra_specs=(), prefix_inputs=(),
                    prefix_specs=()):
    b, _, s_len, _ = zqk.shape
    t = min(ATTN_TILE, s_len)
    hq = 2 * mixer * HEADS_PER_GROUP
    hk = hq + HEADS_PER_GROUP
    hv = mixer * HEADS_PER_GROUP
    hs = HEADS_PER_STEP
    seq_spec = lambda h0: pl.BlockSpec((1, hs, s_len, HEAD_DIM), lambda bi, h: (bi, h0 // hs + h, 0, 0))

    def body(*refs):
        for hh in range(hs):
            kernel_fn(hh, *refs, t=t)

    return pl.pallas_call(
        body,
        grid=(b, HEADS_PER_GROUP // hs),
        in_specs=[*prefix_specs, seq_spec(hq), seq_spec(hk),
                  pl.BlockSpec((1, hs, HEAD_DIM, s_len), lambda bi, h: (bi, hv // hs + h, 0, 0)), *extra_specs],
        out_specs=seq_spec(0),
        out_shape=jax.ShapeDtypeStruct((b, HEADS_PER_GROUP, s_len, HEAD_DIM), BF16),
        compiler_params=_params("arbitrary", "arbitrary"),
        name=name,
    )(*prefix_inputs, zqk, zqk, zvt, *extra_inputs)


def _fox_attention(zqk, zvt, cq, ck):
    s_len = zqk.shape[2]
    aug_spec = pl.BlockSpec((1, HEADS_PER_STEP, s_len, LANES), lambda bi, h: (bi, h, 0, 0))
    return _attention_call(_fox_kernel, "fox", zqk, zvt, MIXER_FOX, (cq, ck), (aug_spec, aug_spec))


def _moba_attention(zqk, zvt):
    s_len = zqk.shape[2]
    assert ATTN_TILE == MOBA_BLOCK and s_len % MOBA_BLOCK == 0 and s_len // MOBA_BLOCK <= BF16_ROWS
    return _attention_call(_moba_kernel, "moba", zqk, zvt, MIXER_MOBA)


def _diff_attention(zqk, zvt, lq1, lk1, lq2, lk2, gsub, lam_init):
    vec = lambda n: pl.BlockSpec((1, n), lambda bi, h: (0, 0))
    return _attention_call(
        _diff_kernel, "diff", zqk, zvt, MIXER_DIFF, (lq1, lk1, lq2, lk2, gsub),
        (vec(DIFF_QK_DIM), vec(DIFF_QK_DIM), vec(DIFF_QK_DIM), vec(DIFF_QK_DIM), vec(HEAD_DIM)),
        prefix_inputs=(jnp.full((1,), lam_init, F32),),
        prefix_specs=(pl.BlockSpec(memory_space=pltpu.SMEM),))


def _sb_attention(zqk, zvt):
    return _attention_call(_sb_kernel, "stickbreak", zqk, zvt, MIXER_SB)


def _cast_weights_once(w_ref, w_scr):
    @pl.when(pl.program_id(1) == 0)
    def _():
        w_scr[...] = w_ref[...].astype(BF16)


def _outproj_norm_kernel(a0_ref, a1_ref, a2_ref, a3_ref, w_ref, x_ref, g_ref, o_ref, h_ref):
    tm = x_ref.shape[0]
    for r in range(tm // MM_CHUNK):
        rows = slice(r * MM_CHUNK, (r + 1) * MM_CHUNK)
        heads = [a_ref[0, s, rows, :] for a_ref in (a0_ref, a1_ref, a2_ref, a3_ref)
                 for s in range(HEADS_PER_GROUP)]
        mixed = jnp.concatenate(heads, axis=-1)
        y = x_ref[rows, :] + _dot(mixed, w_ref[...])
        o_ref[rows, :] = y
        h_ref[rows, :] = _rms_normalize(y, g_ref[...], NORM_EPS).astype(BF16)


def _outproj_norm(mixed, w_bf16, layer, x2, g, s_len):
    m, d = x2.shape
    tm = min(NORM_ROWS, s_len)
    assert tm % MM_CHUNK == 0
    npos = s_len // tm
    a_spec = pl.BlockSpec((1, HEADS_PER_GROUP, tm, HEAD_DIM), lambda i: (i // npos, 0, i % npos, 0))
    row_spec = pl.BlockSpec((tm, d), lambda i: (i, 0))
    return pl.pallas_call(
        _outproj_norm_kernel,
        grid=(m // tm,),
        in_specs=[a_spec, a_spec, a_spec, a_spec,
                  pl.BlockSpec((None, d, d), lambda i: (layer, 0, 0)),
                  row_spec,
                  pl.BlockSpec((1, d), lambda i: (0, 0))],
        out_specs=[row_spec, row_spec],
        out_shape=[jax.ShapeDtypeStruct((m, d), F32), jax.ShapeDtypeStruct((m, d), BF16)],
        compiler_params=_params("arbitrary"),
        name="outproj_norm",
    )(*mixed, w_bf16, x2, g)


def _ffn_up_kernel(h_ref, wg_ref, wu_ref, o_ref, wg_scr, wu_scr):
    _cast_weights_once(wg_ref, wg_scr)
    _cast_weights_once(wu_ref, wu_scr)
    chunk = min(FFN_UP_CHUNK, h_ref.shape[0])
    for r in range(h_ref.shape[0] // chunk):
        rows = slice(r * chunk, (r + 1) * chunk)
        h = h_ref[rows, :]
        gate = _dot(h, wg_scr[...])
        up = _dot(h, wu_scr[...])
        o_ref[rows, :] = (gate / (1.0 + jnp.exp(-gate)) * up).astype(BF16)


def _ffn_up(h2, wg, wu, layer, tn=512):
    m, d = h2.shape
    f = wg.shape[2]
    tm = min(FFN_UP_ROWS, m)
    w_spec = pl.BlockSpec((None, d, tn), lambda j, i: (layer, 0, j))
    return pl.pallas_call(
        _ffn_up_kernel,
        grid=(f // tn, m // tm),
        in_specs=[pl.BlockSpec((tm, d), lambda j, i: (i, 0)), w_spec, w_spec],
        out_specs=pl.BlockSpec((tm, tn), lambda j, i: (i, j)),
        out_shape=jax.ShapeDtypeStruct((m, f), BF16),
        scratch_shapes=[pltpu.VMEM((d, tn), BF16), pltpu.VMEM((d, tn), BF16)],
        compiler_params=_params("arbitrary", "arbitrary"),
        name="ffn_up",
    )(h2, wg, wu)


def _ffn_down_kernel(a_ref, w_ref, x_ref, o_ref, w_scr):
    _cast_weights_once(w_ref, w_scr)
    o_ref[...] = x_ref[...] + _dot(a_ref[...], w_scr[...])


def _ffn_down(act, w, layer, x2, tn=1024):
    m, d = x2.shape
    f = act.shape[1]
    tm = min(DOWN_ROWS, m)
    return pl.pallas_call(
        _ffn_down_kernel,
        grid=(d // tn, m // tm),
        in_specs=[pl.BlockSpec((tm, f), lambda j, i: (i, 0)),
                  pl.BlockSpec((None, f, tn), lambda j, i: (layer, 0, j), pipeline_mode=pl.Buffered(1)),
                  pl.BlockSpec((tm, tn), lambda j, i: (i, j))],
        out_specs=pl.BlockSpec((tm, tn), lambda j, i: (i, j)),
        out_shape=jax.ShapeDtypeStruct((m, d), F32),
        scratch_shapes=[pltpu.VMEM((f, tn), BF16)],
        compiler_params=_params("arbitrary", "arbitrary"),
        name="ffn_down",
    )(act, w, x2)


def _rope_tables(s_len):
    def cos_sin(dim):
        inv = 1.0 / (ROPE_THETA ** (jnp.arange(0, dim, 2, dtype=F32) / dim))
        ang = jnp.arange(s_len, dtype=F32)[:, None] * inv[None, :]
        return jnp.cos(ang), jnp.sin(ang)

    cf, sf = cos_sin(HEAD_DIM)
    ch, sh = cos_sin(DIFF_QK_DIM)
    zh = jnp.zeros_like(sh)
    return (jnp.concatenate([cf, cf], axis=-1),
            jnp.concatenate([-sf, sf], axis=-1),
            jnp.concatenate([ch, ch, ch, ch], axis=-1),
            jnp.concatenate([-sh, zh, -sh, zh], axis=-1),
            jnp.concatenate([zh, sh, zh, sh], axis=-1))


def kernel(x, w_in, b_fgate, w_out, diff_lq1, diff_lk1, diff_lq2, diff_lk2, diff_subln, attn_norm,
           w_gate, w_up, w_down, ffn_norm, final_norm):
    b, s_len, d = x.shape
    depth = w_in.shape[0]
    m = b * s_len
    tables = _rope_tables(s_len)
    b_fg = jnp.pad(b_fgate, ((0, 0), (0, LANES - HEADS_PER_GROUP)))
    w_out_bf16 = w_out.astype(BF16)
    w_in = w_in.astype(BF16)

    x2 = x.reshape(m, d)
    for l in range(depth):
        h, cq, ck = _norm_gate(x2.reshape(b, s_len, d), attn_norm[l][None, :], w_in, l, b_fg[l][None, :])
        h2 = h.reshape(m, d)
        zqk, zvt = _inproj(h2, w_in, l, tables, b, s_len)
        lam_init = 0.8 - 0.6 * math.exp(-0.3 * l)
        fox = _fox_attention(zqk, zvt, cq, ck)
        moba = _moba_attention(zqk, zvt)
        diff = _diff_attention(zqk, zvt, diff_lq1[l][None, :], diff_lk1[l][None, :], diff_lq2[l][None, :],
                               diff_lk2[l][None, :], diff_subln[l][None, :], lam_init)
        sb = _sb_attention(zqk, zvt)
        x2, hf = _outproj_norm((fox, moba, diff, sb), w_out_bf16, l, x2, ffn_norm[l][None, :], s_len)
        act = _ffn_up(hf, w_gate, w_up, l)
        x2 = _ffn_down(act, w_down, l, x2)
    return _norm(x2, final_norm[None, :], F32).reshape(b, s_len, d)
```
